```python
import jax, jax.numpy as jnp
from jax import lax
import numpy as np

D_MODEL = 2048
BATCH = 4
SEQ = 2048
DEPTH = 4

HEAD_DIM = 64
N_HEADS_A = 16
N_HEADS_B = 16
N_KV_B = 2
REP_B = N_HEADS_B // N_KV_B
D_A = N_HEADS_A * HEAD_DIM
D_B = N_HEADS_B * HEAD_DIM
D_KV_B = N_KV_B * HEAD_DIM
D_MIX = D_A + D_B
IN_COLS = 3 * D_A + D_B + 2 * D_KV_B
N_ALIBI = N_HEADS_A + N_HEADS_B
DILATED_CFG = ((128, 1), (512, 4), (2048, 16))
SWA_WINDOW = 128
BLK = 128
D_FF = -(-(8 * D_MODEL) // (3 * 256)) * 256
EPS = 1e-6

kernel_name = "hybrid_dilated_swa_sink_alibi"


def rmsnorm(x, g):
    xf = x.astype(jnp.float32)
    y = xf * lax.rsqrt(jnp.mean(xf * xf, axis=-1, keepdims=True) + EPS)
    return (y * g.astype(jnp.float32)).astype(x.dtype)


def alibi_slopes():
    idx = jnp.arange(N_ALIBI, dtype=jnp.float32)
    s = jnp.exp2(-8.0 * (idx + 1.0) / N_ALIBI)
    return s[0::2], s[1::2]


def banded_parts(q, k, v, slopes, max_dist, pos_scale):
    n, g, r, L, dh = q.shape
    nb = -(-L // BLK)
    pad = nb * BLK - L
    q = jnp.pad(q, ((0, 0), (0, 0), (0, 0), (0, pad), (0, 0)))
    k = jnp.pad(k, ((0, 0), (0, 0), (BLK, pad), (0, 0)))
    v = jnp.pad(v, ((0, 0), (0, 0), (BLK, pad), (0, 0)))
    qb = q.reshape(n, g, r, nb, BLK, dh)
    kb = k.reshape(n, g, nb + 1, BLK, dh)
    vb = v.reshape(n, g, nb + 1, BLK, dh)
    kw = jnp.concatenate([kb[:, :, :-1], kb[:, :, 1:]], axis=3)
    vw = jnp.concatenate([vb[:, :, :-1], vb[:, :, 1:]], axis=3)
    scores = jnp.einsum('ngrbqd,ngbkd->ngrbqk', qb, kw).astype(jnp.float32) * (dh ** -0.5)
    qi = jnp.arange(BLK)[:, None]
    ki = jnp.arange(2 * BLK)[None, :]
    dist = qi + BLK - ki
    key_pos = jnp.arange(nb)[:, None, None] * BLK + ki[None] - BLK
    valid = (dist >= 0) & (dist <= max_dist) & (key_pos >= 0)
    bias = -slopes.astype(jnp.float32)[:, :, None, None, None] * (dist * pos_scale).astype(jnp.float32)
    logits = jnp.where(valid, scores + bias, -jnp.inf)
    m = jnp.max(logits, axis=-1, keepdims=True)
    p = jnp.exp(logits - m)
    s = jnp.sum(p, axis=-1, keepdims=True)
    num = jnp.einsum('ngrbqk,ngbkd->ngrbqd', p, vw.astype(jnp.float32))
    num = num.reshape(n, g, r, nb * BLK, dh)[:, :, :, :L]
    m = m.reshape(n, g, r, nb * BLK, 1)[:, :, :, :L]
    s = s.reshape(n, g, r, nb * BLK, 1)[:, :, :, :L]
    return num, m, s


def dilated_attention(q, k, v, slopes):
    b, h, s_len, dh = q.shape
    nums, ms, ss = [], [], []
    for window, d in DILATED_CFG:
        ls = s_len // d

        def to_sub(t):
            return t.reshape(b, h, ls, d, dh).transpose(3, 0, 1, 2, 4).reshape(d * b, h, ls, dh)

        def from_sub(t):
            c = t.shape[-1]
            return t[:, :, 0].reshape(d, b, h, ls, c).transpose(1, 2, 3, 0, 4).reshape(b, h, s_len, c)

        num, m, sm = banded_parts(to_sub(q)[:, :, None], to_sub(k), to_sub(v),
                                  slopes[:, None], window // d, d)
        nums.append(from_sub(num))
        ms.append(from_sub(m))
        ss.append(from_sub(sm))
    m_all = jnp.maximum(jnp.maximum(ms[0], ms[1]), ms[2])
    w = [jnp.exp(mi - m_all) for mi in ms]
    numer = nums[0] * w[0] + nums[1] * w[1] + nums[2] * w[2]
    denom = ss[0] * w[0] + ss[1] * w[1] + ss[2] * w[2]
    return numer / denom


def swa_sink_attention(q, k, v, slopes, sinks):
    num, m, s = banded_parts(q, k, v, slopes, SWA_WINDOW - 1, 1)
    sink = sinks.astype(jnp.float32)[None, :, :, None, None]
    m2 = jnp.maximum(m, sink)
    c = jnp.exp(m - m2)
    return (num * c) / (s * c + jnp.exp(sink - m2))


def setup_inputs(seed: int = 0) -> dict:
    key = jax.random.key(seed)
    ks = jax.random.split(key, 12)
    f32 = jnp.float32
    x = jax.random.normal(ks[0], (BATCH, SEQ, D_MODEL), f32)
    attn_norm = 1.0 + 0.05 * jax.random.normal(ks[1], (DEPTH, D_MODEL), f32)
    w_in = jax.random.normal(ks[2], (DEPTH, D_MODEL, IN_COLS), f32) * D_MODEL ** -0.5
    sinks = 0.5 * jax.random.normal(ks[3], (DEPTH, N_HEADS_B), f32)
    out_norm_a = 1.0 + 0.05 * jax.random.normal(ks[4], (DEPTH, D_A), f32)
    out_norm_b = 1.0 + 0.05 * jax.random.normal(ks[5], (DEPTH, D_B), f32)
    w_out = jax.random.normal(ks[6], (DEPTH, D_MIX, D_MODEL), f32) * D_MIX ** -0.5
    ffn_norm = 1.0 + 0.05 * jax.random.normal(ks[7], (DEPTH, D_MODEL), f32)
    w_gate = jax.random.normal(ks[8], (DEPTH, D_MODEL, D_FF), f32) * D_MODEL ** -0.5
    w_up = jax.random.normal(ks[9], (DEPTH, D_MODEL, D_FF), f32) * D_MODEL ** -0.5
    w_down = jax.random.normal(ks[10], (DEPTH, D_FF, D_MODEL), f32) * D_FF ** -0.5
    final_norm = 1.0 + 0.05 * jax.random.normal(ks[11], (D_MODEL,), f32)
    return {"x": x, "attn_norm": attn_norm, "w_in": w_in, "sinks": sinks,
            "out_norm_a": out_norm_a, "out_norm_b": out_norm_b, "w_out": w_out,
            "ffn_norm": ffn_norm, "w_gate": w_gate, "w_up": w_up, "w_down": w_down,
            "final_norm": final_norm}


def reference(x, attn_norm, w_in, sinks, out_norm_a, out_norm_b, w_out,
              ffn_norm, w_gate, w_up, w_down, final_norm):
    b, s_len, _ = x.shape
    slopes_a, slopes_b = alibi_slopes()
    slopes_b = slopes_b.reshape(N_KV_B, REP_B)
    for l in range(DEPTH):
        h = rmsnorm(x, attn_norm[l])
        proj = jnp.einsum('bsd,de->bse', h, w_in[l])
        c0, c1, c2 = D_A, 2 * D_A, 3 * D_A
        c3, c4 = c2 + D_B, c2 + D_B + D_KV_B
        qa = proj[..., :c0].reshape(b, s_len, N_HEADS_A, HEAD_DIM).transpose(0, 2, 1, 3)
        ka = proj[..., c0:c1].reshape(b, s_len, N_HEADS_A, HEAD_DIM).transpose(0, 2, 1, 3)
        va = proj[..., c1:c2].reshape(b, s_len, N_HEADS_A, HEAD_DIM).transpose(0, 2, 1, 3)
        qb = proj[..., c2:c3].reshape(b, s_len, N_KV_B, REP_B, HEAD_DIM).transpose(0, 2, 3, 1, 4)
        kb = proj[..., c3:c4].reshape(b, s_len, N_KV_B, HEAD_DIM).transpose(0, 2, 1, 3)
        vb = proj[..., c4:].reshape(b, s_len, N_KV_B, HEAD_DIM).transpose(0, 2, 1, 3)

        out_a = dilated_attention(qa, ka, va, slopes_a)
        out_a = out_a.transpose(0, 2, 1, 3).reshape(b, s_len, D_A).astype(x.dtype)
        out_b = swa_sink_attention(qb, kb, vb, slopes_b, sinks[l].reshape(N_KV_B, REP_B))
        out_b = out_b.transpose(0, 3, 1, 2, 4).reshape(b, s_len, D_B).astype(x.dtype)
        mix = jnp.concatenate([rmsnorm(out_a, out_norm_a[l]),
                               rmsnorm(out_b, out_norm_b[l])], axis=-1)
        x = x + jnp.einsum('bse,ed->bsd', mix, w_out[l])
        h = rmsnorm(x, ffn_norm[l])
        gate = jnp.einsum('bsd,df->bsf', h, w_gate[l])
        up = jnp.einsum('bsd,df->bsf', h, w_up[l])
        x = x + jnp.einsum('bsf,fd->bsd', jax.nn.silu(gate) * up, w_down[l])
    return rmsnorm(x, final_norm)
```

```python
import functools

import numpy as np
import jax
import jax.numpy as jnp
from jax import lax
from jax.experimental import pallas as pl
from jax.experimental.pallas import tpu as pltpu

F32 = jnp.float32
BF16 = jnp.bfloat16

D_MODEL = 2048
SEQ = 2048
HEAD_DIM = 64
N_HEADS_A = 16
N_HEADS_B = 16
N_KV_B = 2
D_A = N_HEADS_A * HEAD_DIM
D_B = N_HEADS_B * HEAD_DIM
D_KV_B = N_KV_B * HEAD_DIM
D_MIX = D_A + D_B
IN_COLS = 3 * D_A + D_B + 2 * D_KV_B
N_ALIBI = N_HEADS_A + N_HEADS_B
D_FF = 5632
EPS = 1e-6
SWA_WINDOW = 128
DIL_MAX_DIST = 128

LANES = 128
BLK = 128
N_RES = 16
N_J = SEQ // N_RES
N_BLK = SEQ // BLK
VMEM_LIMIT = 56 * 1024 * 1024


def _rms(x, g):
    return x * lax.rsqrt(jnp.mean(x * x, axis=-1, keepdims=True) + EPS) * g


def _dist_patterns():
    inf = np.float32(np.inf)
    j = np.arange(BLK)
    d16 = (j[:, None] - j[None, :]).astype(np.float32)
    d16 = np.where(d16 >= 0, 16.0 * d16, inf).astype(np.float32)

    t = np.arange(BLK) // 32
    jj = np.arange(BLK) % 32
    sub = 4 * jj + t
    cur = sub[:, None] - sub[None, :]
    prev = cur + BLK
    d4 = np.concatenate([prev, cur], axis=1).astype(np.float32)
    d4 = np.where((d4 >= 0) & (d4 <= DIL_MAX_DIST), 4.0 * d4, inf).astype(np.float32)

    r = np.arange(BLK) // 8
    jj = np.arange(BLK) % 8
    pos = 16 * jj + r
    cur = pos[:, None] - pos[None, :]
    prev = cur + BLK
    d1 = np.concatenate([prev, cur], axis=1).astype(np.float32)
    d1a = np.where((d1 >= 0) & (d1 <= DIL_MAX_DIST), d1, inf).astype(np.float32)
    d1b = np.where((d1 >= 0) & (d1 <= SWA_WINDOW - 1), d1, inf).astype(np.float32)
    return d16, d4, d1a, d1b


def _norm_permute_kernel(x_ref, g_ref, xp_ref, h_ref):
    x = x_ref[...]
    xp_ref[...] = x
    h_ref[...] = _rms(x, g_ref[...]).astype(BF16)


def _norm_permute(x, g):
    b = x.shape[0]
    xv = x.reshape(b, N_J, N_RES * D_MODEL)
    xp, h = pl.pallas_call(
        _norm_permute_kernel,
        grid=(b, N_RES),
        in_specs=[pl.BlockSpec((None, N_J, D_MODEL), lambda i, r: (i, 0, r)),
                  pl.BlockSpec((1, D_MODEL), lambda i, r: (0, 0))],
        out_specs=[pl.BlockSpec((None, None, N_J, D_MODEL), lambda i, r: (i, r, 0, 0)),
                   pl.BlockSpec((None, None, N_J, D_MODEL), lambda i, r: (i, r, 0, 0))],
        out_shape=[jax.ShapeDtypeStruct((b, N_RES, N_J, D_MODEL), F32),
                   jax.ShapeDtypeStruct((b, N_RES, N_J, D_MODEL), BF16)],
        compiler_params=pltpu.CompilerParams(
            dimension_semantics=("arbitrary", "arbitrary"), vmem_limit_bytes=VMEM_LIMIT),
        name="norm_permute",
    )(xv, g.reshape(1, D_MODEL))
    return xp.reshape(b * SEQ, D_MODEL), h.reshape(b * SEQ, D_MODEL)


def _final_norm_kernel(x_ref, g_ref, o_ref):
    o_ref[...] = _rms(x_ref[...], g_ref[...])


def _final_norm_unpermute(xp, g, b):
    out = pl.pallas_call(
        _final_norm_kernel,
        grid=(b, N_RES),
        in_specs=[pl.BlockSpec((None, None, N_J, D_MODEL), lambda i, r: (i, r, 0, 0)),
                  pl.BlockSpec((1, D_MODEL), lambda i, r: (0, 0))],
        out_specs=pl.BlockSpec((None, N_J, D_MODEL), lambda i, r: (i, 0, r)),
        out_shape=jax.ShapeDtypeStruct((b, N_J, N_RES * D_MODEL), F32),
        compiler_params=pltpu.CompilerParams(
            dimension_semantics=("arbitrary", "arbitrary"), vmem_limit_bytes=VMEM_LIMIT),
        name="final_norm",
    )(xp.reshape(b, N_RES, N_J, D_MODEL), g.reshape(1, D_MODEL))
    return out.reshape(b, SEQ, D_MODEL)


def _matmul_kernel(a_ref, w_ref, o_ref):
    o_ref[...] = jnp.dot(a_ref[...], w_ref[...], preferred_element_type=F32)


def _in_proj(h, w):
    t = h.shape[0]
    tm, tn = 512, IN_COLS // 2
    return pl.pallas_call(
        _matmul_kernel,
        grid=(IN_COLS // tn, t // tm),
        in_specs=[pl.BlockSpec((tm, D_MODEL), lambda n, m: (m, 0)),
                  pl.BlockSpec((D_MODEL, tn), lambda n, m: (0, n))],
        out_specs=pl.BlockSpec((tm, tn), lambda n, m: (m, n)),
        out_shape=jax.ShapeDtypeStruct((t, IN_COLS), F32),
        compiler_params=pltpu.CompilerParams(
            dimension_semantics=("arbitrary", "arbitrary"), vmem_limit_bytes=VMEM_LIMIT),
        name="in_proj",
    )(h, w)


def _lane_lo():
    return lax.broadcasted_iota(jnp.int32, (BLK, LANES), 1) < HEAD_DIM


def _attend(q, kk, vv, bias0, bias1):
    nk = kk.shape[0] // 2
    s = lax.dot_general(q, kk, (((1,), (1,)), ((), ())), preferred_element_type=F32)
    l0 = s[:, :nk] + bias0
    l1 = s[:, nk:] + bias1
    m0 = jnp.max(l0, axis=1, keepdims=True)
    m1 = jnp.max(l1, axis=1, keepdims=True)
    p0 = jnp.exp(l0 - m0)
    p1 = jnp.exp(l1 - m1)
    s0 = jnp.sum(p0, axis=1, keepdims=True)
    s1 = jnp.sum(p1, axis=1, keepdims=True)
    p = jnp.concatenate([p0, p1], axis=1).astype(BF16)
    num = jnp.dot(p, vv, preferred_element_type=F32)
    return num, m0, s0, m1, s1


def _rows(ref, lead, start, size):
    return ref[lead, pl.ds(start, size), :]


def _kv_window(scr, start, size):
    return jnp.concatenate([scr[0, pl.ds(start, size), :], scr[1, pl.ds(start, size), :]], axis=0)


def _gather_rows(ref, starts, size):
    return jnp.concatenate([ref[pl.ds(s, size), :] for s in starts], axis=0)


def _store_split(dst, x_lo, x_hi, rows):
    dst[0, rows, :] = x_lo.astype(BF16)
    dst[1, rows, :] = x_hi.astype(BF16)


def _attn_a_kernel(slopes_ref, q_ref, k_ref, v_ref, d16_ref, d4_ref, d1_ref, o_ref,
                   q16, q4, q1, k16, k4, k1, v16, v4, v1, b16, b4, b1,
                   num_s, m_s, s_s):
    hp = pl.program_id(1)
    lo = _lane_lo()
    ns0 = -slopes_ref[2 * hp]
    ns1 = -slopes_ref[2 * hp + 1]

    b16[0] = ns0 * d16_ref[...]
    b16[1] = ns1 * d16_ref[...]
    b4[0] = ns0 * d4_ref[...]
    b4[1] = ns1 * d4_ref[...]
    b1[0] = ns0 * d1_ref[...]
    b1[1] = ns1 * d1_ref[...]

    def split(x):
        return jnp.where(lo, x, 0.0), jnp.where(lo, 0.0, x)

    def build(g, carry):
        dst = pl.ds(pl.multiple_of(g * BLK, BLK), BLK)
        src16 = [pl.multiple_of(g * BLK, BLK)]
        r4 = g // 4
        mb = g % 4
        src4 = [pl.multiple_of((r4 + 4 * t) * N_J + 32 * mb, 32) for t in range(4)]
        src1 = [pl.multiple_of(r * N_J + 8 * g, 8) for r in range(N_RES)]
        for srcs, size, qd, kd, vd in ((src16, BLK, q16, k16, v16),
                                       (src4, 32, q4, k4, v4),
                                       (src1, 8, q1, k1, v1)):
            qd[dst, :] = (_gather_rows(q_ref, srcs, size) * (HEAD_DIM ** -0.5)).astype(BF16)
            _store_split(kd, *split(_gather_rows(k_ref, srcs, size)), dst)
            _store_split(vd, *split(_gather_rows(v_ref, srcs, size)), dst)
        return carry

    lax.fori_loop(0, N_BLK, build, 0)

    def emit(branch, num, m0, s0, m1, s1, dsts, size):
        mm = jnp.where(lo, m0, m1)
        ss = jnp.where(lo, s0, s1)
        for i, d in enumerate(dsts):
            rows = pl.ds(d, size)
            src = slice(i * size, (i + 1) * size)
            num_s[branch, rows, :] = num[src]
            m_s[branch, rows, :] = mm[src]
            s_s[branch, rows, :] = ss[src]

    def blk16(g, carry):
        rows = pl.multiple_of(g * BLK, BLK)
        res = _attend(q16[pl.ds(rows, BLK), :], _kv_window(k16, rows, BLK),
                      _kv_window(v16, rows, BLK), b16[0], b16[1])
        emit(2, *res, [rows], BLK)
        return carry

    lax.fori_loop(0, N_BLK, blk16, 0)

    def banded(g, first, qs, ks, vs, bias, branch, dsts, size):
        rows = pl.multiple_of(g * BLK, BLK)
        q = qs[pl.ds(rows, BLK), :]
        if first:
            res = _attend(q, _kv_window(ks, rows, BLK), _kv_window(vs, rows, BLK),
                          bias[0, :, BLK:], bias[1, :, BLK:])
        else:
            prev = pl.multiple_of(rows - BLK, BLK)
            res = _attend(q, _kv_window(ks, prev, 2 * BLK), _kv_window(vs, prev, 2 * BLK),
                          bias[0], bias[1])
        emit(branch, *res, dsts, size)

    def dst4(g):
        r4 = g // 4
        mb = g % 4
        return [pl.multiple_of((r4 + 4 * t) * N_J + 32 * mb, 32) for t in range(4)]

    def dst1(g):
        return [pl.multiple_of(r * N_J + 8 * g, 8) for r in range(N_RES)]

    for r4 in range(4):
        banded(4 * r4, True, q4, k4, v4, b4, 1, dst4(4 * r4), 32)

        def blk4(mb, carry, r4=r4):
            g = 4 * r4 + mb
            banded(g, False, q4, k4, v4, b4, 1, dst4(g), 32)
            return carry

        lax.fori_loop(1, 4, blk4, 0)

    banded(0, True, q1, k1, v1, b1, 0, dst1(0), 8)

    def blk1(g, carry):
        banded(g, False, q1, k1, v1, b1, 0, dst1(g), 8)
        return carry

    lax.fori_loop(1, N_BLK, blk1, 0)

    def merge(g, carry):
        rows = pl.ds(pl.multiple_of(g * BLK, BLK), BLK)
        m = [m_s[i, rows, :] for i in range(3)]
        m_all = jnp.maximum(jnp.maximum(m[0], m[1]), m[2])
        w = [jnp.exp(mi - m_all) for mi in m]
        numer = num_s[0, rows, :] * w[0] + num_s[1, rows, :] * w[1] + num_s[2, rows, :] * w[2]
        denom = s_s[0, rows, :] * w[0] + s_s[1, rows, :] * w[1] + s_s[2, rows, :] * w[2]
        o_ref[rows, :] = numer / denom
        return carry

    lax.fori_loop(0, N_BLK, merge, 0)


def _attn_a(proj3, slopes_a, d16, d4, d1a):
    b = proj3.shape[0]
    n_pairs = N_HEADS_A // 2
    col = lambda off: pl.BlockSpec((None, SEQ, LANES), lambda i, hp, off=off: (i, 0, off + hp))
    const = lambda shape: pl.BlockSpec(shape, lambda i, hp: (0, 0))
    bf = lambda *shape: pltpu.VMEM(shape, BF16)
    return pl.pallas_call(
        _attn_a_kernel,
        grid=(b, n_pairs),
        in_specs=[pl.BlockSpec(memory_space=pltpu.SMEM),
                  col(0), col(D_A // LANES), col(2 * D_A // LANES),
                  const((BLK, BLK)), const((BLK, 2 * BLK)), const((BLK, 2 * BLK))],
        out_specs=pl.BlockSpec((None, SEQ, LANES), lambda i, hp: (i, 0, hp)),
        out_shape=jax.ShapeDtypeStruct((b, SEQ, D_A), F32),
        scratch_shapes=[bf(SEQ, LANES), bf(SEQ, LANES), bf(SEQ, LANES),
                        bf(2, SEQ, LANES), bf(2, SEQ, LANES), bf(2, SEQ, LANES),
                        bf(2, SEQ, LANES), bf(2, SEQ, LANES), bf(2, SEQ, LANES),
                        pltpu.VMEM((2, BLK, BLK), F32),
                        pltpu.VMEM((2, BLK, 2 * BLK), F32),
                        pltpu.VMEM((2, BLK, 2 * BLK), F32),
                        pltpu.VMEM((3, SEQ, LANES), F32),
                        pltpu.VMEM((3, SEQ, LANES), F32),
                        pltpu.VMEM((3, SEQ, LANES), F32)],
        compiler_params=pltpu.CompilerParams(
            dimension_semantics=("arbitrary", "arbitrary"), vmem_limit_bytes=VMEM_LIMIT),
        name="attn_dilated",
    )(slopes_a, proj3, proj3, proj3, d16, d4, d1a)


def _attn_b_kernel(slopes_ref, sinks_ref, q_ref, k_ref, v_ref, d1_ref, o_ref,
                   q1, k1, v1, b1):
    hp = pl.program_id(1)
    kv_is0 = hp < (N_HEADS_B // N_KV_B) // 2
    lo = _lane_lo()
    ns0 = -slopes_ref[2 * hp]
    ns1 = -slopes_ref[2 * hp + 1]
    sink0 = sinks_ref[2 * hp]
    sink1 = sinks_ref[2 * hp + 1]
    b1[0] = ns0 * d1_ref[...]
    b1[1] = ns1 * d1_ref[...]

    def split_kv(x):
        xr = pltpu.roll(x, HEAD_DIM, 1)
        first = jnp.where(kv_is0, x, xr)
        second = jnp.where(kv_is0, xr, x)
        return jnp.where(lo, first, 0.0), jnp.where(lo, 0.0, second)

    def build(g, carry):
        dst = pl.ds(pl.multiple_of(g * BLK, BLK), BLK)
        src1 = [pl.multiple_of(r * N_J + 8 * g, 8) for r in range(N_RES)]
        q1[dst, :] = (_gather_rows(q_ref, src1, 8) * (HEAD_DIM ** -0.5)).astype(BF16)
        _store_split(k1, *split_kv(_gather_rows(k_ref, src1, 8)), dst)
        _store_split(v1, *split_kv(_gather_rows(v_ref, src1, 8)), dst)
        return carry

    lax.fori_loop(0, N_BLK, build, 0)

    def block(g, first):
        rows = pl.multiple_of(g * BLK, BLK)
        q = q1[pl.ds(rows, BLK), :]
        if first:
            num, m0, s0, m1, s1 = _attend(
                q, _kv_window(k1, rows, BLK), _kv_window(v1, rows, BLK),
                b1[0, :, BLK:], b1[1, :, BLK:])
        else:
            prev = pl.multiple_of(rows - BLK, BLK)
            num, m0, s0, m1, s1 = _attend(
                q, _kv_window(k1, prev, 2 * BLK), _kv_window(v1, prev, 2 * BLK),
                b1[0], b1[1])

        def factor(m, s, sink):
            m2 = jnp.maximum(m, sink)
            c = jnp.exp(m - m2)
            return c, s * c + jnp.exp(sink - m2)

        c0, den0 = factor(m0, s0, sink0)
        c1, den1 = factor(m1, s1, sink1)
        out = (num * jnp.where(lo, c0, c1)) / jnp.where(lo, den0, den1)
        for r in range(N_RES):
            o_ref[pl.ds(pl.multiple_of(r * N_J + 8 * g, 8), 8), :] = out[r * 8:(r + 1) * 8]

    block(0, True)

    def body(g, carry):
        block(g, False)
        return carry

    lax.fori_loop(1, N_BLK, body, 0)


def _attn_b(proj3, slopes_b, sinks, d1b):
    b = proj3.shape[0]
    n_pairs = N_HEADS_B // 2
    q_off = 3 * D_A // LANES
    k_off = (3 * D_A + D_B) // LANES
    v_off = k_off + 1
    return pl.pallas_call(
        _attn_b_kernel,
        grid=(b, n_pairs),
        in_specs=[pl.BlockSpec(memory_space=pltpu.SMEM),
                  pl.BlockSpec(memory_space=pltpu.SMEM),
                  pl.BlockSpec((None, SEQ, LANES), lambda i, hp: (i, 0, q_off + hp)),
                  pl.BlockSpec((None, SEQ, LANES), lambda i, hp: (i, 0, k_off)),
                  pl.BlockSpec((None, SEQ, LANES), lambda i, hp: (i, 0, v_off)),
                  pl.BlockSpec((BLK, 2 * BLK), lambda i, hp: (0, 0))],
        out_specs=pl.BlockSpec((None, SEQ, LANES), lambda i, hp: (i, 0, hp)),
        out_shape=jax.ShapeDtypeStruct((b, SEQ, D_B), F32),
        scratch_shapes=[pltpu.VMEM((SEQ, LANES), BF16),
                        pltpu.VMEM((2, SEQ, LANES), BF16),
                        pltpu.VMEM((2, SEQ, LANES), BF16),
                        pltpu.VMEM((2, BLK, 2 * BLK), F32)],
        compiler_params=pltpu.CompilerParams(
            dimension_semantics=("arbitrary", "arbitrary"), vmem_limit_bytes=VMEM_LIMIT),
        name="attn_swa",
    )(slopes_b, sinks, proj3, proj3, proj3, d1b)


def _out_proj_kernel(a_ref, b_ref, x_ref, ga_ref, gb_ref, w_ref, gf_ref, xo_ref, h_ref):
    na = _rms(a_ref[...], ga_ref[...]).astype(BF16)
    nb = _rms(b_ref[...], gb_ref[...]).astype(BF16)
    mix = jnp.concatenate([na, nb], axis=1)
    x = x_ref[...] + jnp.dot(mix, w_ref[...], preferred_element_type=F32)
    xo_ref[...] = x
    h_ref[...] = _rms(x, gf_ref[...]).astype(BF16)


def _out_proj(a, b, x, ga, gb, w, gf):
    t = x.shape[0]
    tm = 512
    row = lambda n: pl.BlockSpec((tm, n), lambda m: (m, 0))
    const = lambda r, c: pl.BlockSpec((r, c), lambda m: (0, 0))
    return pl.pallas_call(
        _out_proj_kernel,
        grid=(t // tm,),
        in_specs=[row(D_A), row(D_B), row(D_MODEL), const(1, D_A), const(1, D_B),
                  const(D_MIX, D_MODEL), const(1, D_MODEL)],
        out_specs=[row(D_MODEL), row(D_MODEL)],
        out_shape=[jax.ShapeDtypeStruct((t, D_MODEL), F32),
                   jax.ShapeDtypeStruct((t, D_MODEL), BF16)],
        compiler_params=pltpu.CompilerParams(
            dimension_semantics=("arbitrary",), vmem_limit_bytes=VMEM_LIMIT),
        name="out_proj",
    )(a, b, x, ga.reshape(1, D_A), gb.reshape(1, D_B), w, gf.reshape(1, D_MODEL))


def _ffn_kernel(h_ref, wg_ref, wu_ref, wd_ref, x_ref, gn_ref, xo_ref, hn_ref, acc_ref):
    f = pl.program_id(1)
    h = h_ref[...]
    gate = jnp.dot(h, wg_ref[...], preferred_element_type=F32)
    up = jnp.dot(h, wu_ref[...], preferred_element_type=F32)
    act = (jax.nn.silu(gate) * up).astype(BF16)
    part = jnp.dot(act, wd_ref[...], preferred_element_type=F32)

    @pl.when(f == 0)
    def _():
        acc_ref[...] = part

    @pl.when(f > 0)
    def _():
        acc_ref[...] += part

    @pl.when(f == pl.num_programs(1) - 1)
    def _():
        x = x_ref[...] + acc_ref[...]
        xo_ref[...] = x
        hn_ref[...] = _rms(x, gn_ref[...]).astype(BF16)


def _ffn(h, wg, wu, wd, x, gn):
    t = x.shape[0]
    tm, tf = 512, 512
    return pl.pallas_call(
        _ffn_kernel,
        grid=(t // tm, D_FF // tf),
        in_specs=[pl.BlockSpec((tm, D_MODEL), lambda m, f: (m, 0)),
                  pl.BlockSpec((D_MODEL, tf), lambda m, f: (0, f)),
                  pl.BlockSpec((D_MODEL, tf), lambda m, f: (0, f)),
                  pl.BlockSpec((tf, D_MODEL), lambda m, f: (f, 0)),
                  pl.BlockSpec((tm, D_MODEL), lambda m, f: (m, 0)),
                  pl.BlockSpec((1, D_MODEL), lambda m, f: (0, 0))],
        out_specs=[pl.BlockSpec((tm, D_MODEL), lambda m, f: (m, 0)),
                   pl.BlockSpec((tm, D_MODEL), lambda m, f: (m, 0))],
        out_shape=[jax.ShapeDtypeStruct((t, D_MODEL), F32),
                   jax.ShapeDtypeStruct((t, D_MODEL), BF16)],
        scratch_shapes=[pltpu.VMEM((tm, D_MODEL), F32)],
        compiler_params=pltpu.CompilerParams(
            dimension_semantics=("arbitrary", "arbitrary"), vmem_limit_bytes=VMEM_LIMIT),
        name="ffn",
    )(h, wg, wu, wd, x, gn.reshape(1, D_MODEL))


def kernel(x, attn_norm, w_in, sinks, out_norm_a, out_norm_b, w_out,
           ffn_norm, w_gate, w_up, w_down, final_norm):
    b, s_len, d = x.shape
    assert (s_len, d) == (SEQ, D_MODEL)
    depth = w_in.shape[0]

    idx = jnp.arange(N_ALIBI, dtype=F32)
    slopes = jnp.exp2(-8.0 * (idx + 1.0) / N_ALIBI)
    slopes_a, slopes_b = slopes[0::2], slopes[1::2]
    d16, d4, d1a, d1b = (jnp.asarray(p) for p in _dist_patterns())

    xp, h = _norm_permute(x, attn_norm[0])
    for l in range(depth):
        proj = _in_proj(h, w_in[l].astype(BF16)).reshape(b, SEQ, IN_COLS)
        out_a = _attn_a(proj, slopes_a, d16, d4, d1a).reshape(b * SEQ, D_A)
        out_b = _attn_b(proj, slopes_b, sinks[l], d1b).reshape(b * SEQ, D_B)
        xp, h = _out_proj(out_a, out_b, xp, out_norm_a[l], out_norm_b[l],
                          w_out[l].astype(BF16), ffn_norm[l])
        g_next = attn_norm[l + 1] if l + 1 < depth else final_norm
        xp, h = _ffn(h, w_gate[l].astype(BF16), w_up[l].astype(BF16),
                     w_down[l].astype(BF16), xp, g_next)
    return _final_norm_unpermute(xp, final_norm, b)
```

```python
import math

import numpy as np
import jax
import jax.numpy as jnp
from jax import lax
from jax.experimental import pallas as pl
from jax.experimental.pallas import tpu as pltpu

F32 = jnp.float32
BF16 = jnp.bfloat16

D_MODEL = 2048
SEQ = 2048
HEAD_DIM = 64
N_HEADS_A = 16
N_HEADS_B = 16
N_KV_B = 2
D_A = N_HEADS_A * HEAD_DIM
D_B = N_HEADS_B * HEAD_DIM
D_KV_B = N_KV_B * HEAD_DIM
D_MIX = D_A + D_B
IN_COLS = 3 * D_A + D_B + 2 * D_KV_B
N_ALIBI = N_HEADS_A + N_HEADS_B
D_FF = 5632
EPS = 1e-6
SWA_WINDOW = 128
DIL_MAX_DIST = 128

LANES = 128
BLK = 128
N_RES = 16
N_J = SEQ // N_RES
N_BLK = SEQ // BLK
GROUP = 4
N_GROUPS = N_BLK // GROUP
LOG2E = math.log2(math.e)
VMEM_LIMIT = 56 * 1024 * 1024


def _rms(x, g):
    return x * lax.rsqrt(jnp.mean(x * x, axis=-1, keepdims=True) + EPS) * g


def _dist_patterns():
    inf = np.float32(np.inf)
    j = np.arange(BLK)
    d16 = (j[:, None] - j[None, :]).astype(np.float32)
    d16 = np.where(d16 >= 0, 16.0 * d16, inf).astype(np.float32)

    t = np.arange(BLK) // 32
    jj = np.arange(BLK) % 32
    sub = 4 * jj + t
    cur = sub[:, None] - sub[None, :]
    prev = cur + BLK
    d4 = np.concatenate([prev, cur], axis=1).astype(np.float32)
    d4 = np.where((d4 >= 0) & (d4 <= DIL_MAX_DIST), 4.0 * d4, inf).astype(np.float32)

    r = np.arange(BLK) // 8
    jj = np.arange(BLK) % 8
    pos = 16 * jj + r
    cur = pos[:, None] - pos[None, :]
    prev = cur + BLK
    d1 = np.concatenate([prev, cur], axis=1).astype(np.float32)
    d1a = np.where((d1 >= 0) & (d1 <= DIL_MAX_DIST), d1, inf).astype(np.float32)
    d1b = np.where((d1 >= 0) & (d1 <= SWA_WINDOW - 1), d1, inf).astype(np.float32)
    return d16, d4, d1a, d1b


def _ones_pattern(nk):
    e = np.zeros((2 * nk, LANES), np.float32)
    e[:nk, :HEAD_DIM] = 1.0
    e[nk:, HEAD_DIM:] = 1.0
    return e


def _norm_permute_kernel(x_ref, g_ref, xp_ref, h_ref):
    x = x_ref[...]
    xp_ref[...] = x
    h_ref[...] = _rms(x, g_ref[...]).astype(BF16)


def _norm_permute(x, g):
    b = x.shape[0]
    xv = x.reshape(b, N_J, N_RES * D_MODEL)
    xp, h = pl.pallas_call(
        _norm_permute_kernel,
        grid=(b, N_RES),
        in_specs=[pl.BlockSpec((None, N_J, D_MODEL), lambda i, r: (i, 0, r)),
                  pl.BlockSpec((1, D_MODEL), lambda i, r: (0, 0))],
        out_specs=[pl.BlockSpec((None, None, N_J, D_MODEL), lambda i, r: (i, r, 0, 0)),
                   pl.BlockSpec((None, None, N_J, D_MODEL), lambda i, r: (i, r, 0, 0))],
        out_shape=[jax.ShapeDtypeStruct((b, N_RES, N_J, D_MODEL), F32),
                   jax.ShapeDtypeStruct((b, N_RES, N_J, D_MODEL), BF16)],
        compiler_params=pltpu.CompilerParams(
            dimension_semantics=("arbitrary", "arbitrary"), vmem_limit_bytes=VMEM_LIMIT),
        name="norm_permute",
    )(xv, g.reshape(1, D_MODEL))
    return xp.reshape(b * SEQ, D_MODEL), h.reshape(b * SEQ, D_MODEL)


def _final_norm_kernel(x_ref, g_ref, o_ref):
    o_ref[...] = _rms(x_ref[...], g_ref[...])


def _final_norm_unpermute(xp, g, b):
    out = pl.pallas_call(
        _final_norm_kernel,
        grid=(b, N_RES),
        in_specs=[pl.BlockSpec((None, None, N_J, D_MODEL), lambda i, r: (i, r, 0, 0)),
                  pl.BlockSpec((1, D_MODEL), lambda i, r: (0, 0))],
        out_specs=pl.BlockSpec((None, N_J, D_MODEL), lambda i, r: (i, 0, r)),
        out_shape=jax.ShapeDtypeStruct((b, N_J, N_RES * D_MODEL), F32),
        compiler_params=pltpu.CompilerParams(
            dimension_semantics=("arbitrary", "arbitrary"), vmem_limit_bytes=VMEM_LIMIT),
        name="final_norm",
    )(xp.reshape(b, N_RES, N_J, D_MODEL), g.reshape(1, D_MODEL))
    return out.reshape(b, SEQ, D_MODEL)


def _matmul_kernel(a_ref, w_ref, o_ref):
    o_ref[...] = jnp.dot(a_ref[...], w_ref[...], preferred_element_type=F32)


def _in_proj(h, w):
    t = h.shape[0]
    tm, tn = 512, IN_COLS // 2
    return pl.pallas_call(
        _matmul_kernel,
        grid=(IN_COLS // tn, t // tm),
        in_specs=[pl.BlockSpec((tm, D_MODEL), lambda n, m: (m, 0)),
                  pl.BlockSpec((D_MODEL, tn), lambda n, m: (0, n))],
        out_specs=pl.BlockSpec((tm, tn), lambda n, m: (m, n)),
        out_shape=jax.ShapeDtypeStruct((t, IN_COLS), F32),
        compiler_params=pltpu.CompilerParams(
            dimension_semantics=("arbitrary", "arbitrary"), vmem_limit_bytes=VMEM_LIMIT),
        name="in_proj",
    )(h, w)


def _lane_lo():
    return lax.broadcasted_iota(jnp.int32, (BLK, LANES), 1) < HEAD_DIM


def _attend_group(q, kk, vv, bias0, bias1):
    nk = kk.shape[1] // 2
    s = jnp.einsum("gqd,gkd->gqk", q, kk, preferred_element_type=F32)
    l0 = s[:, :, :nk] + bias0
    l1 = s[:, :, nk:] + bias1
    m0 = jnp.max(l0, axis=2, keepdims=True)
    m1 = jnp.max(l1, axis=2, keepdims=True)
    p = jnp.concatenate([jnp.exp2(l0 - m0), jnp.exp2(l1 - m1)], axis=2).astype(BF16)
    na = jnp.einsum("gqk,gkd->gqd", p, vv, preferred_element_type=F32)
    return na[:, :, :LANES], na[:, :, LANES:], m0, m1


def _kv_window(scr, start, size):
    return jnp.concatenate([scr[0, pl.ds(start, size), :], scr[1, pl.ds(start, size), :]], axis=0)


def _group_operands(qs, ks, vs, ones, q_starts, k_starts, nk):
    q = jnp.stack([qs[pl.ds(s, BLK), :] for s in q_starts])
    kk = jnp.stack([_kv_window(ks, s, nk) for s in k_starts])
    vv = jnp.stack([jnp.concatenate([_kv_window(vs, s, nk), ones], axis=1) for s in k_starts])
    return q, kk, vv


def _gather_rows(ref, starts, size):
    return jnp.concatenate([ref[pl.ds(s, size), :] for s in starts], axis=0)


def _store_split(dst, x_lo, x_hi, rows):
    dst[0, rows, :] = x_lo.astype(BF16)
    dst[1, rows, :] = x_hi.astype(BF16)


def _first_block_bias(bias, bias_first):
    for h in range(2):
        bias_first[h, :, :BLK] = bias[h, :, BLK:]
        bias_first[h, :, BLK:] = jnp.full((BLK, BLK), -jnp.inf, F32)


Q_SCALE = LOG2E * HEAD_DIM ** -0.5


def _attn_a_kernel(slopes_ref, q_ref, k_ref, v_ref, d16_ref, d4_ref, d1_ref, e1_ref, e2_ref,
                   o_ref, q16, q4, q1, k16, k4, k1, v16, v4, v1, b16, b4, b4f, b1, b1f,
                   num_s, m_s, den_s):
    hp = pl.program_id(1)
    lo = _lane_lo()
    ns0 = -LOG2E * slopes_ref[2 * hp]
    ns1 = -LOG2E * slopes_ref[2 * hp + 1]

    for bias, dist in ((b16, d16_ref), (b4, d4_ref), (b1, d1_ref)):
        bias[0] = ns0 * dist[...]
        bias[1] = ns1 * dist[...]
    _first_block_bias(b4, b4f)
    _first_block_bias(b1, b1f)

    def split(x):
        return jnp.where(lo, x, 0.0), jnp.where(lo, 0.0, x)

    def build(g, carry):
        dst = pl.ds(pl.multiple_of(g * BLK, BLK), BLK)
        src16 = [pl.multiple_of(g * BLK, BLK)]
        r4 = g // 4
        mb = g % 4
        src4 = [pl.multiple_of((r4 + 4 * t) * N_J + 32 * mb, 32) for t in range(4)]
        src1 = [pl.multiple_of(r * N_J + 8 * g, 8) for r in range(N_RES)]
        for srcs, size, qd, kd, vd in ((src16, BLK, q16, k16, v16),
                                       (src4, 32, q4, k4, v4),
                                       (src1, 8, q1, k1, v1)):
            qd[dst, :] = (_gather_rows(q_ref, srcs, size) * Q_SCALE).astype(BF16)
            _store_split(kd, *split(_gather_rows(k_ref, srcs, size)), dst)
            _store_split(vd, *split(_gather_rows(v_ref, srcs, size)), dst)
        return carry

    lax.fori_loop(0, N_BLK, build, 0, unroll=2)

    def emit(branch, i, res, dsts, size):
        num, den, m0, m1 = res
        mm = jnp.where(lo, m0[i], m1[i])
        for c, d in enumerate(dsts):
            rows = pl.ds(d, size)
            src = slice(c * size, (c + 1) * size)
            num_s[branch, rows, :] = num[i, src]
            den_s[branch, rows, :] = den[i, src]
            m_s[branch, rows, :] = mm[src]

    def blocks_of(it):
        return [pl.multiple_of((it * GROUP + i) * BLK, BLK) for i in range(GROUP)]

    def grp16(it, carry):
        rows = blocks_of(it)
        res = _attend_group(*_group_operands(q16, k16, v16, e1_ref[...], rows, rows, BLK),
                            b16[0], b16[1])
        for i in range(GROUP):
            emit(2, i, res, [rows[i]], BLK)
        return carry

    lax.fori_loop(0, N_GROUPS, grp16, 0)

    def banded(it, first, qs, ks, vs, bias, bias_first, branch, dsts, size):
        rows = blocks_of(it)
        k_starts = [rows[0] if (first and i == 0) else pl.multiple_of(rows[i] - BLK, BLK)
                    for i in range(GROUP)]
        if first:
            bias0, bias1 = (jnp.stack([bias_first[h]] + [bias[h]] * (GROUP - 1))
                            for h in range(2))
        else:
            bias0, bias1 = bias[0], bias[1]
        res = _attend_group(
            *_group_operands(qs, ks, vs, e2_ref[...], rows, k_starts, 2 * BLK), bias0, bias1)
        for i in range(GROUP):
            emit(branch, i, res, dsts(it, i), size)

    def dst4(r4, mb):
        return [pl.multiple_of((r4 + 4 * t) * N_J + 32 * mb, 32) for t in range(4)]

    def dst1(it, i):
        g = it * GROUP + i
        return [pl.multiple_of(r * N_J + 8 * g, 8) for r in range(N_RES)]

    def grp4(r4, carry):
        banded(r4, True, q4, k4, v4, b4, b4f, 1, dst4, 32)
        return carry

    lax.fori_loop(0, N_GROUPS, grp4, 0)

    banded(0, True, q1, k1, v1, b1, b1f, 0, dst1, 8)

    def grp1(it, carry):
        banded(it, False, q1, k1, v1, b1, b1f, 0, dst1, 8)
        return carry

    lax.fori_loop(1, N_GROUPS, grp1, 0)

    def merge(g, carry):
        rows = pl.ds(pl.multiple_of(g * BLK, BLK), BLK)
        m = [m_s[i, rows, :] for i in range(3)]
        m_all = jnp.maximum(jnp.maximum(m[0], m[1]), m[2])
        w = [jnp.exp2(mi - m_all) for mi in m]
        numer = num_s[0, rows, :] * w[0] + num_s[1, rows, :] * w[1] + num_s[2, rows, :] * w[2]
        denom = den_s[0, rows, :] * w[0] + den_s[1, rows, :] * w[1] + den_s[2, rows, :] * w[2]
        o_ref[rows, :] = numer / denom
        return carry

    lax.fori_loop(0, N_BLK, merge, 0, unroll=2)


def _attn_a(proj3, slopes_a, d16, d4, d1a, e1, e2):
    b = proj3.shape[0]
    n_pairs = N_HEADS_A // 2
    col = lambda off: pl.BlockSpec((None, SEQ, LANES), lambda i, hp, off=off: (i, 0, off + hp))
    const = lambda shape: pl.BlockSpec(shape, lambda i, hp: (0, 0))
    bf = lambda *shape: pltpu.VMEM(shape, BF16)
    band = pltpu.VMEM((2, BLK, 2 * BLK), F32)
    return pl.pallas_call(
        _attn_a_kernel,
        grid=(b, n_pairs),
        in_specs=[pl.BlockSpec(memory_space=pltpu.SMEM),
                  col(0), col(D_A // LANES), col(2 * D_A // LANES),
                  const((BLK, BLK)), const((BLK, 2 * BLK)), const((BLK, 2 * BLK)),
                  const((2 * BLK, LANES)), const((4 * BLK, LANES))],
        out_specs=pl.BlockSpec((None, SEQ, LANES), lambda i, hp: (i, 0, hp)),
        out_shape=jax.ShapeDtypeStruct((b, SEQ, D_A), F32),
        scratch_shapes=[bf(SEQ, LANES), bf(SEQ, LANES), bf(SEQ, LANES),
                        bf(2, SEQ, LANES), bf(2, SEQ, LANES), bf(2, SEQ, LANES),
                        bf(2, SEQ, LANES), bf(2, SEQ, LANES), bf(2, SEQ, LANES),
                        pltpu.VMEM((2, BLK, BLK), F32), band, band, band, band,
                        pltpu.VMEM((3, SEQ, LANES), F32),
                        pltpu.VMEM((3, SEQ, LANES), F32),
                        pltpu.VMEM((3, SEQ, LANES), F32)],
        compiler_params=pltpu.CompilerParams(
            dimension_semantics=("arbitrary", "arbitrary"), vmem_limit_bytes=VMEM_LIMIT),
        name="attn_dilated",
    )(slopes_a, proj3, proj3, proj3, d16, d4, d1a, e1, e2)


def _attn_b_kernel(slopes_ref, sinks_ref, q_ref, k_ref, v_ref, d1_ref, e2_ref, o_ref,
                   q1, k1, v1, b1, b1f):
    hp = pl.program_id(1)
    kv_is0 = hp < (N_HEADS_B // N_KV_B) // 2
    lo = _lane_lo()
    sink0 = LOG2E * sinks_ref[2 * hp]
    sink1 = LOG2E * sinks_ref[2 * hp + 1]
    b1[0] = (-LOG2E * slopes_ref[2 * hp]) * d1_ref[...]
    b1[1] = (-LOG2E * slopes_ref[2 * hp + 1]) * d1_ref[...]
    _first_block_bias(b1, b1f)

    def split_kv(x):
        xr = pltpu.roll(x, HEAD_DIM, 1)
        first = jnp.where(kv_is0, x, xr)
        second = jnp.where(kv_is0, xr, x)
        return jnp.where(lo, first, 0.0), jnp.where(lo, 0.0, second)

    def build(g, carry):
        dst = pl.ds(pl.multiple_of(g * BLK, BLK), BLK)
        src1 = [pl.multiple_of(r * N_J + 8 * g, 8) for r in range(N_RES)]
        q1[dst, :] = (_gather_rows(q_ref, src1, 8) * Q_SCALE).astype(BF16)
        _store_split(k1, *split_kv(_gather_rows(k_ref, src1, 8)), dst)
        _store_split(v1, *split_kv(_gather_rows(v_ref, src1, 8)), dst)
        return carry

    lax.fori_loop(0, N_BLK, build, 0, unroll=2)

    def factor(m, sink):
        m2 = jnp.maximum(m, sink)
        return jnp.exp2(m - m2), jnp.exp2(sink - m2)

    def group(it, first):
        rows = [pl.multiple_of((it * GROUP + i) * BLK, BLK) for i in range(GROUP)]
        k_starts = [rows[0] if (first and i == 0) else pl.multiple_of(rows[i] - BLK, BLK)
                    for i in range(GROUP)]
        if first:
            bias0, bias1 = (jnp.stack([b1f[h]] + [b1[h]] * (GROUP - 1)) for h in range(2))
        else:
            bias0, bias1 = b1[0], b1[1]
        num, den, m0, m1 = _attend_group(
            *_group_operands(q1, k1, v1, e2_ref[...], rows, k_starts, 2 * BLK), bias0, bias1)
        c0, t0 = factor(m0, sink0)
        c1, t1 = factor(m1, sink1)
        for i in range(GROUP):
            c = jnp.where(lo, c0[i], c1[i])
            out = (num[i] * c) / (den[i] * c + jnp.where(lo, t0[i], t1[i]))
            g = it * GROUP + i
            for r in range(N_RES):
                o_ref[pl.ds(pl.multiple_of(r * N_J + 8 * g, 8), 8), :] = out[r * 8:(r + 1) * 8]

    group(0, True)

    def body(it, carry):
        group(it, False)
        return carry

    lax.fori_loop(1, N_GROUPS, body, 0)


def _attn_b(proj3, slopes_b, sinks, d1b, e2):
    b = proj3.shape[0]
    n_pairs = N_HEADS_B // 2
    q_off = 3 * D_A // LANES
    k_off = (3 * D_A + D_B) // LANES
    v_off = k_off + 1
    band = pltpu.VMEM((2, BLK, 2 * BLK), F32)
    return pl.pallas_call(
        _attn_b_kernel,
        grid=(b, n_pairs),
        in_specs=[pl.BlockSpec(memory_space=pltpu.SMEM),
                  pl.BlockSpec(memory_space=pltpu.SMEM),
                  pl.BlockSpec((None, SEQ, LANES), lambda i, hp: (i, 0, q_off + hp)),
                  pl.BlockSpec((None, SEQ, LANES), lambda i, hp: (i, 0, k_off)),
                  pl.BlockSpec((None, SEQ, LANES), lambda i, hp: (i, 0, v_off)),
                  pl.BlockSpec((BLK, 2 * BLK), lambda i, hp: (0, 0)),
                  pl.BlockSpec((4 * BLK, LANES), lambda i, hp: (0, 0))],
        out_specs=pl.BlockSpec((None, SEQ, LANES), lambda i, hp: (i, 0, hp)),
        out_shape=jax.ShapeDtypeStruct((b, SEQ, D_B), F32),
        scratch_shapes=[pltpu.VMEM((SEQ, LANES), BF16),
                        pltpu.VMEM((2, SEQ, LANES), BF16),
                        pltpu.VMEM((2, SEQ, LANES), BF16),
                        band, band],
        compiler_params=pltpu.CompilerParams(
            dimension_semantics=("arbitrary", "arbitrary"), vmem_limit_bytes=VMEM_LIMIT),
        name="attn_swa",
    )(slopes_b, sinks, proj3, proj3, proj3, d1b, e2)


def _out_proj_kernel(a_ref, b_ref, x_ref, ga_ref, gb_ref, w_ref, gf_ref, xo_ref, h_ref):
    na = _rms(a_ref[...], ga_ref[...]).astype(BF16)
    nb = _rms(b_ref[...], gb_ref[...]).astype(BF16)
    mix = jnp.concatenate([na, nb], axis=1)
    x = x_ref[...] + jnp.dot(mix, w_ref[...], preferred_element_type=F32)
    xo_ref[...] = x
    h_ref[...] = _rms(x, gf_ref[...]).astype(BF16)


def _out_proj(a, b, x, ga, gb, w, gf):
    t = x.shape[0]
    tm = 512
    row = lambda n: pl.BlockSpec((tm, n), lambda m: (m, 0))
    const = lambda r, c: pl.BlockSpec((r, c), lambda m: (0, 0))
    return pl.pallas_call(
        _out_proj_kernel,
        grid=(t // tm,),
        in_specs=[row(D_A), row(D_B), row(D_MODEL), const(1, D_A), const(1, D_B),
                  const(D_MIX, D_MODEL), const(1, D_MODEL)],
        out_specs=[row(D_MODEL), row(D_MODEL)],
        out_shape=[jax.ShapeDtypeStruct((t, D_MODEL), F32),
                   jax.ShapeDtypeStruct((t, D_MODEL), BF16)],
        compiler_params=pltpu.CompilerParams(
            dimension_semantics=("arbitrary",), vmem_limit_bytes=VMEM_LIMIT),
        name="out_proj",
    )(a, b, x, ga.reshape(1, D_A), gb.reshape(1, D_B), w, gf.reshape(1, D_MODEL))


def _ffn_kernel(h_ref, wg_ref, wu_ref, wd_ref, x_ref, gn_ref, xo_ref, hn_ref, acc_ref):
    f = pl.program_id(1)

    @pl.when(f == 0)
    def _():
        acc_ref[...] = jnp.zeros_like(acc_ref)

    h = h_ref[...]
    gate = jnp.dot(h, wg_ref[...], preferred_element_type=F32)
    up = jnp.dot(h, wu_ref[...], preferred_element_type=F32)
    act = (jax.nn.silu(gate) * up).astype(BF16)
    acc_ref[...] += jnp.dot(act, wd_ref[...], preferred_element_type=F32)

    @pl.when(f == pl.num_programs(1) - 1)
    def _():
        x = x_ref[...] + acc_ref[...]
        xo_ref[...] = x
        hn_ref[...] = _rms(x, gn_ref[...]).astype(BF16)


def _ffn(h, wg, wu, wd, x, gn):
    t = x.shape[0]
    tm, tf = 512, 512
    return pl.pallas_call(
        _ffn_kernel,
        grid=(t // tm, D_FF // tf),
        in_specs=[pl.BlockSpec((tm, D_MODEL), lambda m, f: (m, 0)),
                  pl.BlockSpec((D_MODEL, tf), lambda m, f: (0, f)),
                  pl.BlockSpec((D_MODEL, tf), lambda m, f: (0, f)),
                  pl.BlockSpec((tf, D_MODEL), lambda m, f: (f, 0)),
                  pl.BlockSpec((tm, D_MODEL), lambda m, f: (m, 0)),
                  pl.BlockSpec((1, D_MODEL), lambda m, f: (0, 0))],
        out_specs=[pl.BlockSpec((tm, D_MODEL), lambda m, f: (m, 0)),
                   pl.BlockSpec((tm, D_MODEL), lambda m, f: (m, 0))],
        out_shape=[jax.ShapeDtypeStruct((t, D_MODEL), F32),
                   jax.ShapeDtypeStruct((t, D_MODEL), BF16)],
        scratch_shapes=[pltpu.VMEM((tm, D_MODEL), F32)],
        compiler_params=pltpu.CompilerParams(
            dimension_semantics=("arbitrary", "arbitrary"), vmem_limit_bytes=VMEM_LIMIT),
        name="ffn",
    )(h, wg, wu, wd, x, gn.reshape(1, D_MODEL))


def kernel(x, attn_norm, w_in, sinks, out_norm_a, out_norm_b, w_out,
           ffn_norm, w_gate, w_up, w_down, final_norm):
    b, s_len, d = x.shape
    assert (s_len, d) == (SEQ, D_MODEL)
    depth = w_in.shape[0]

    idx = jnp.arange(N_ALIBI, dtype=F32)
    slopes = jnp.exp2(-8.0 * (idx + 1.0) / N_ALIBI)
    slopes_a, slopes_b = slopes[0::2], slopes[1::2]
    d16, d4, d1a, d1b = (jnp.asarray(p) for p in _dist_patterns())
    e1 = jnp.asarray(_ones_pattern(BLK), BF16)
    e2 = jnp.asarray(_ones_pattern(2 * BLK), BF16)

    xp, h = _norm_permute(x, attn_norm[0])
    for l in range(depth):
        proj = _in_proj(h, w_in[l].astype(BF16)).reshape(b, SEQ, IN_COLS)
        out_a = _attn_a(proj, slopes_a, d16, d4, d1a, e1, e2).reshape(b * SEQ, D_A)
        out_b = _attn_b(proj, slopes_b, sinks[l], d1b, e2).reshape(b * SEQ, D_B)
        xp, h = _out_proj(out_a, out_b, xp, out_norm_a[l], out_norm_b[l],
                          w_out[l].astype(BF16), ffn_norm[l])
        g_next = attn_norm[l + 1] if l + 1 < depth else final_norm
        xp, h = _ffn(h, w_gate[l].astype(BF16), w_up[l].astype(BF16),
                     w_down[l].astype(BF16), xp, g_next)
    return _final_norm_unpermute(xp, final_norm, b)
```

```python
import math

import numpy as np
import jax
import jax.numpy as jnp
from jax import lax
from jax.experimental import pallas as pl
from jax.experimental.pallas import tpu as pltpu

F32 = jnp.float32
BF16 = jnp.bfloat16

D_MODEL = 2048
SEQ = 2048
HEAD_DIM = 64
N_HEADS_A = 16
N_HEADS_B = 16
N_KV_B = 2
D_A = N_HEADS_A * HEAD_DIM
D_B = N_HEADS_B * HEAD_DIM
D_KV_B = N_KV_B * HEAD_DIM
D_MIX = D_A + D_B
IN_COLS = 3 * D_A + D_B + 2 * D_KV_B
N_ALIBI = N_HEADS_A + N_HEADS_B
D_FF = 5632
EPS = 1e-6
SWA_WINDOW = 128
DIL_MAX_DIST = 128

LANES = 128
BLK = 128
N_RES = 16
N_J = SEQ // N_RES
N_BLK = SEQ // BLK
GROUP = 4
N_GROUPS = N_BLK // GROUP
LOG2E = math.log2(math.e)
VMEM_LIMIT = 56 * 1024 * 1024


def _rms(x, g):
    return x * lax.rsqrt(jnp.mean(x * x, axis=-1, keepdims=True) + EPS) * g


def _dist_patterns():
    inf = np.float32(np.inf)
    j = np.arange(BLK)
    d16 = (j[:, None] - j[None, :]).astype(np.float32)
    d16 = np.where(d16 >= 0, 16.0 * d16, inf).astype(np.float32)

    t = np.arange(BLK) // 32
    jj = np.arange(BLK) % 32
    sub = 4 * jj + t
    cur = sub[:, None] - sub[None, :]
    prev = cur + BLK
    d4 = np.concatenate([prev, cur], axis=1).astype(np.float32)
    d4 = np.where((d4 >= 0) & (d4 <= DIL_MAX_DIST), 4.0 * d4, inf).astype(np.float32)

    r = np.arange(BLK) // 8
    jj = np.arange(BLK) % 8
    pos = 16 * jj + r
    cur = pos[:, None] - pos[None, :]
    prev = cur + BLK
    d1 = np.concatenate([prev, cur], axis=1).astype(np.float32)
    d1a = np.where((d1 >= 0) & (d1 <= DIL_MAX_DIST), d1, inf).astype(np.float32)
    d1b = np.where((d1 >= 0) & (d1 <= SWA_WINDOW - 1), d1, inf).astype(np.float32)
    return d16, d4, d1a, d1b


def _ones_pattern(nk):
    e = np.zeros((2 * nk, LANES), np.float32)
    e[:nk, :HEAD_DIM] = 1.0
    e[nk:, HEAD_DIM:] = 1.0
    return e


def _norm_permute_kernel(x_ref, g_ref, xp_ref, h_ref):
    x = x_ref[...]
    xp_ref[...] = x
    h_ref[...] = _rms(x, g_ref[...]).astype(BF16)


def _norm_permute(x, g):
    b = x.shape[0]
    xv = x.reshape(b, N_J, N_RES * D_MODEL)
    xp, h = pl.pallas_call(
        _norm_permute_kernel,
        grid=(b, N_RES),
        in_specs=[pl.BlockSpec((None, N_J, D_MODEL), lambda i, r: (i, 0, r)),
                  pl.BlockSpec((1, D_MODEL), lambda i, r: (0, 0))],
        out_specs=[pl.BlockSpec((None, None, N_J, D_MODEL), lambda i, r: (i, r, 0, 0)),
                   pl.BlockSpec((None, None, N_J, D_MODEL), lambda i, r: (i, r, 0, 0))],
        out_shape=[jax.ShapeDtypeStruct((b, N_RES, N_J, D_MODEL), F32),
                   jax.ShapeDtypeStruct((b, N_RES, N_J, D_MODEL), BF16)],
        compiler_params=pltpu.CompilerParams(
            dimension_semantics=("arbitrary", "arbitrary"), vmem_limit_bytes=VMEM_LIMIT),
        name="norm_permute",
    )(xv, g.reshape(1, D_MODEL))
    return xp.reshape(b * SEQ, D_MODEL), h.reshape(b * SEQ, D_MODEL)


def _final_norm_kernel(x_ref, g_ref, o_ref):
    o_ref[...] = _rms(x_ref[...], g_ref[...])


def _final_norm_unpermute(xp, g, b):
    out = pl.pallas_call(
        _final_norm_kernel,
        grid=(b, N_RES),
        in_specs=[pl.BlockSpec((None, None, N_J, D_MODEL), lambda i, r: (i, r, 0, 0)),
                  pl.BlockSpec((1, D_MODEL), lambda i, r: (0, 0))],
        out_specs=pl.BlockSpec((None, N_J, D_MODEL), lambda i, r: (i, 0, r)),
        out_shape=jax.ShapeDtypeStruct((b, N_J, N_RES * D_MODEL), F32),
        compiler_params=pltpu.CompilerParams(
            dimension_semantics=("arbitrary", "arbitrary"), vmem_limit_bytes=VMEM_LIMIT),
        name="final_norm",
    )(xp.reshape(b, N_RES, N_J, D_MODEL), g.reshape(1, D_MODEL))
    return out.reshape(b, SEQ, D_MODEL)


IN_PROJ_RES = 2
IN_PROJ_TM = IN_PROJ_RES * N_J
IN_PROJ_STEPS = N_RES // IN_PROJ_RES
Q_SCALE = LOG2E * HEAD_DIM ** -0.5


def _in_proj_kernel(h_ref, w_ref, q16, k16l, k16h, v16l, v16h, q4, k4l, k4h, v4l, v4h,
                    q1, k1l, k1h, v1l, v1h, qb1, kvb1):
    h = h_ref[...]

    def order1(x):
        n = x.shape[1]
        pieces = [x[r * N_J + jb * 8:r * N_J + jb * 8 + 8]
                  for jb in range(N_J // 8) for r in range(IN_PROJ_RES)]
        return jnp.concatenate(pieces, axis=0).reshape(N_J // 8, 8 * IN_PROJ_RES, n)

    def proj(col, width, scale=None):
        x = jnp.dot(h, w_ref[:, col:col + width], preferred_element_type=F32)
        return x if scale is None else x * scale

    def store3(x, o16, o4, o1):
        o16[...] = x.astype(BF16)
        o4[...] = x.reshape(IN_PROJ_RES, 4, 32, D_A).astype(BF16)
        o1[...] = order1(x).astype(BF16)

    def first_head(width):
        return (lax.broadcasted_iota(jnp.int32, (IN_PROJ_TM, width), 1) & HEAD_DIM) == 0

    store3(proj(0, D_A, Q_SCALE), q16, q4, q1)
    lo = first_head(D_A)
    for i, outs in ((1, ((k16l, k4l, k1l), (k16h, k4h, k1h))),
                    (2, ((v16l, v4l, v1l), (v16h, v4h, v1h)))):
        x = proj(i * D_A, D_A)
        store3(jnp.where(lo, x, 0.0), *outs[0])
        store3(jnp.where(lo, 0.0, x), *outs[1])
    qb1[...] = order1(proj(3 * D_A, D_B, Q_SCALE)).astype(BF16)

    x = proj(3 * D_A + D_B, 2 * D_KV_B)
    lo = first_head(LANES)
    tiles = []
    for part in (x[:, :LANES], x[:, LANES:]):
        swapped = pltpu.roll(part, HEAD_DIM, 1)
        tiles += [jnp.where(lo, part, 0.0), jnp.where(lo, 0.0, swapped),
                  jnp.where(lo, swapped, 0.0), jnp.where(lo, 0.0, part)]
    kvb1[...] = order1(jnp.concatenate(tiles, axis=1)).astype(BF16)


def _in_proj(h, w_all, layer, b):
    t = h.shape[0]
    per = IN_PROJ_STEPS
    o16 = pl.BlockSpec((IN_PROJ_TM, D_A), lambda m: (m, 0))
    o4 = pl.BlockSpec((None, IN_PROJ_RES, 4, None, 32, D_A),
                      lambda m: (m // per, (m % per) % 2, 0, (m % per) // 2, 0, 0))
    o1 = lambda c: pl.BlockSpec((None, N_J // 8, None, 8 * IN_PROJ_RES, c),
                                lambda m: (m // per, 0, m % per, 0, 0))
    s16 = jax.ShapeDtypeStruct((t, D_A), BF16)
    s4 = jax.ShapeDtypeStruct((b, 4, 4, 4, 32, D_A), BF16)
    s1 = lambda c: jax.ShapeDtypeStruct((b, N_J // 8, per, 8 * IN_PROJ_RES, c), BF16)
    outs = pl.pallas_call(
        _in_proj_kernel,
        grid=(t // IN_PROJ_TM,),
        in_specs=[pl.BlockSpec((IN_PROJ_TM, D_MODEL), lambda m: (m, 0)),
                  pl.BlockSpec((None, D_MODEL, IN_COLS), lambda m: (layer, 0, 0),
                               pipeline_mode=pl.Buffered(1))],
        out_specs=[o16] * 5 + [o4] * 5 + [o1(D_A)] * 5 + [o1(D_B), o1(8 * LANES)],
        out_shape=[s16] * 5 + [s4] * 5 + [s1(D_A)] * 5 + [s1(D_B), s1(8 * LANES)],
        compiler_params=pltpu.CompilerParams(
            dimension_semantics=("arbitrary",), vmem_limit_bytes=VMEM_LIMIT),
        name="in_proj",
    )(h, w_all)
    return [o.reshape(b, SEQ, o.shape[-1]) for o in outs]


def _lane_lo():
    return lax.broadcasted_iota(jnp.int32, (BLK, LANES), 1) < HEAD_DIM


def _attend_group(q, kk, vv, bias0, bias1):
    nk = kk.shape[1] // 2
    s = jnp.einsum("gqd,gkd->gqk", q, kk, preferred_element_type=F32)
    l0 = s[:, :, :nk] + bias0
    l1 = s[:, :, nk:] + bias1
    m0 = jnp.max(l0, axis=2, keepdims=True)
    m1 = jnp.max(l1, axis=2, keepdims=True)
    p = jnp.concatenate([jnp.exp2(l0 - m0), jnp.exp2(l1 - m1)], axis=2).astype(BF16)
    na = jnp.einsum("gqk,gkd->gqd", p, vv, preferred_element_type=F32)
    return na[:, :, :LANES], na[:, :, LANES:], m0, m1


def _kv_window(pair, start, size):
    return jnp.concatenate([ref[pl.ds(start, size), :] for ref in pair], axis=0)


def _group_operands(qs, ks, vs, ones, q_starts, k_starts, nk):
    q = jnp.stack([qs[pl.ds(s, BLK), :] for s in q_starts])
    kk = jnp.stack([_kv_window(ks, s, nk) for s in k_starts])
    vv = jnp.stack([jnp.concatenate([_kv_window(vs, s, nk), ones], axis=1) for s in k_starts])
    return q, kk, vv


def _first_block_bias(bias, bias_first):
    for h in range(2):
        bias_first[h, :, :BLK] = bias[h, :, BLK:]
        bias_first[h, :, BLK:] = jnp.full((BLK, BLK), -jnp.inf, F32)


def _attn_a_kernel(slopes_ref, q16, k16l, k16h, v16l, v16h, q4, k4l, k4h, v4l, v4h,
                   q1, k1l, k1h, v1l, v1h, d16_ref, d4_ref, d1_ref, e1_ref, e2_ref,
                   o_ref, b16, b4, b4f, b1, b1f, num_s, m_s, den_s):
    hp = pl.program_id(1)
    lo = _lane_lo()
    ns0 = -LOG2E * slopes_ref[2 * hp]
    ns1 = -LOG2E * slopes_ref[2 * hp + 1]
    k16, v16, k4, v4, k1, v1 = ((k16l, k16h), (v16l, v16h), (k4l, k4h), (v4l, v4h),
                                (k1l, k1h), (v1l, v1h))

    for bias, dist in ((b16, d16_ref), (b4, d4_ref), (b1, d1_ref)):
        bias[0] = ns0 * dist[...]
        bias[1] = ns1 * dist[...]
    _first_block_bias(b4, b4f)
    _first_block_bias(b1, b1f)

    def emit(branch, i, res, dsts, size):
        num, den, m0, m1 = res
        mm = jnp.where(lo, m0[i], m1[i])
        for c, d in enumerate(dsts):
            rows = pl.ds(d, size)
            src = slice(c * size, (c + 1) * size)
            num_s[branch, rows, :] = num[i, src]
            den_s[branch, rows, :] = den[i, src]
            m_s[branch, rows, :] = mm[src]

    def blocks_of(it):
        return [pl.multiple_of((it * GROUP + i) * BLK, BLK) for i in range(GROUP)]

    def grp16(it, carry):
        rows = blocks_of(it)
        res = _attend_group(*_group_operands(q16, k16, v16, e1_ref[...], rows, rows, BLK),
                            b16[0], b16[1])
        for i in range(GROUP):
            emit(2, i, res, [rows[i]], BLK)
        return carry

    lax.fori_loop(0, N_GROUPS, grp16, 0)

    def banded(it, first, qs, ks, vs, bias, bias_first, branch, dsts, size):
        rows = blocks_of(it)
        k_starts = [rows[0] if (first and i == 0) else pl.multiple_of(rows[i] - BLK, BLK)
                    for i in range(GROUP)]
        if first:
            bias0, bias1 = (jnp.stack([bias_first[h]] + [bias[h]] * (GROUP - 1))
                            for h in range(2))
        else:
            bias0, bias1 = bias[0], bias[1]
        res = _attend_group(
            *_group_operands(qs, ks, vs, e2_ref[...], rows, k_starts, 2 * BLK), bias0, bias1)
        for i in range(GROUP):
            emit(branch, i, res, dsts(it, i), size)

    def dst4(r4, mb):
        return [pl.multiple_of((r4 + 4 * t) * N_J + 32 * mb, 32) for t in range(4)]

    def dst1(it, i):
        g = it * GROUP + i
        return [pl.multiple_of(r * N_J + 8 * g, 8) for r in range(N_RES)]

    def grp4(r4, carry):
        banded(r4, True, q4, k4, v4, b4, b4f, 1, dst4, 32)
        return carry

    lax.fori_loop(0, N_GROUPS, grp4, 0)

    banded(0, True, q1, k1, v1, b1, b1f, 0, dst1, 8)

    def grp1(it, carry):
        banded(it, False, q1, k1, v1, b1, b1f, 0, dst1, 8)
        return carry

    lax.fori_loop(1, N_GROUPS, grp1, 0)

    def merge(g, carry):
        rows = pl.ds(pl.multiple_of(g * BLK, BLK), BLK)
        m = [m_s[i, rows, :] for i in range(3)]
        m_all = jnp.maximum(jnp.maximum(m[0], m[1]), m[2])
        w = [jnp.exp2(mi - m_all) for mi in m]
        numer = num_s[0, rows, :] * w[0] + num_s[1, rows, :] * w[1] + num_s[2, rows, :] * w[2]
        denom = den_s[0, rows, :] * w[0] + den_s[1, rows, :] * w[1] + den_s[2, rows, :] * w[2]
        o_ref[rows, :] = numer / denom
        return carry

    lax.fori_loop(0, N_BLK, merge, 0, unroll=2)


def _attn_a(qkv, slopes_a, d16, d4, d1a, e1, e2):
    b = qkv[0].shape[0]
    n_pairs = N_HEADS_A // 2
    col = pl.BlockSpec((None, SEQ, LANES), lambda i, hp: (i, 0, hp))
    const = lambda shape: pl.BlockSpec(shape, lambda i, hp: (0, 0))
    band = pltpu.VMEM((2, BLK, 2 * BLK), F32)
    stat = pltpu.VMEM((3, SEQ, LANES), F32)
    return pl.pallas_call(
        _attn_a_kernel,
        grid=(b, n_pairs),
        in_specs=[pl.BlockSpec(memory_space=pltpu.SMEM)] + [col] * 15 + [
            const((BLK, BLK)), const((BLK, 2 * BLK)), const((BLK, 2 * BLK)),
            const((2 * BLK, LANES)), const((4 * BLK, LANES))],
        out_specs=pl.BlockSpec((None, SEQ, LANES), lambda i, hp: (i, 0, hp)),
        out_shape=jax.ShapeDtypeStruct((b, SEQ, D_A), F32),
        scratch_shapes=[pltpu.VMEM((2, BLK, BLK), F32), band, band, band, band,
                        stat, stat, stat],
        compiler_params=pltpu.CompilerParams(
            dimension_semantics=("arbitrary", "arbitrary"), vmem_limit_bytes=VMEM_LIMIT),
        name="attn_dilated",
    )(slopes_a, *qkv, d16, d4, d1a, e1, e2)


def _attn_b_kernel(slopes_ref, sinks_ref, q1, k1l, k1h, v1l, v1h, d1_ref, e2_ref, o_ref, b1, b1f):
    hp = pl.program_id(1)
    lo = _lane_lo()
    sink0 = LOG2E * sinks_ref[2 * hp]
    sink1 = LOG2E * sinks_ref[2 * hp + 1]
    b1[0] = (-LOG2E * slopes_ref[2 * hp]) * d1_ref[...]
    b1[1] = (-LOG2E * slopes_ref[2 * hp + 1]) * d1_ref[...]
    _first_block_bias(b1, b1f)
    k1, v1 = (k1l, k1h), (v1l, v1h)

    sink = jnp.where(lo, sink0, sink1)

    def group(it, first):
        rows = [pl.multiple_of((it * GROUP + i) * BLK, BLK) for i in range(GROUP)]
        k_starts = [rows[0] if (first and i == 0) else pl.multiple_of(rows[i] - BLK, BLK)
                    for i in range(GROUP)]
        if first:
            bias0, bias1 = (jnp.stack([b1f[h]] + [b1[h]] * (GROUP - 1)) for h in range(2))
        else:
            bias0, bias1 = b1[0], b1[1]
        num, den, m0, m1 = _attend_group(
            *_group_operands(q1, k1, v1, e2_ref[...], rows, k_starts, 2 * BLK), bias0, bias1)
        for i in range(GROUP):
            m = jnp.where(lo, m0[i], m1[i])
            m2 = jnp.maximum(m, sink)
            c = jnp.exp2(m - m2)
            out = (num[i] * c) / (den[i] * c + jnp.exp2(sink - m2))
            g = it * GROUP + i
            for r in range(N_RES):
                o_ref[pl.ds(pl.multiple_of(r * N_J + 8 * g, 8), 8), :] = out[r * 8:(r + 1) * 8]

    group(0, True)

    def body(it, carry):
        group(it, False)
        return carry

    lax.fori_loop(1, N_GROUPS, body, 0)


def _attn_b(q1, kv1, slopes_b, sinks, d1b, e2):
    b = q1.shape[0]
    n_pairs = N_HEADS_B // 2
    pairs_per_kv = n_pairs // N_KV_B
    kv_tile = lambda tile: pl.BlockSpec(
        (None, SEQ, LANES), lambda i, hp, tile=tile: (i, 0, tile + 2 * (hp // pairs_per_kv)))
    band = pltpu.VMEM((2, BLK, 2 * BLK), F32)
    return pl.pallas_call(
        _attn_b_kernel,
        grid=(b, n_pairs),
        in_specs=[pl.BlockSpec(memory_space=pltpu.SMEM),
                  pl.BlockSpec(memory_space=pltpu.SMEM),
                  pl.BlockSpec((None, SEQ, LANES), lambda i, hp: (i, 0, hp)),
                  kv_tile(0), kv_tile(1), kv_tile(4), kv_tile(5),
                  pl.BlockSpec((BLK, 2 * BLK), lambda i, hp: (0, 0)),
                  pl.BlockSpec((4 * BLK, LANES), lambda i, hp: (0, 0))],
        out_specs=pl.BlockSpec((None, SEQ, LANES), lambda i, hp: (i, 0, hp)),
        out_shape=jax.ShapeDtypeStruct((b, SEQ, D_B), F32),
        scratch_shapes=[band, band],
        compiler_params=pltpu.CompilerParams(
            dimension_semantics=("arbitrary", "arbitrary"), vmem_limit_bytes=VMEM_LIMIT),
        name="attn_swa",
    )(slopes_b, sinks, q1, kv1, kv1, kv1, kv1, d1b, e2)


def _out_proj_kernel(a_ref, b_ref, x_ref, ga_ref, gb_ref, w_ref, gf_ref, xo_ref, h_ref):
    na = _rms(a_ref[...], ga_ref[...]).astype(BF16)
    nb = _rms(b_ref[...], gb_ref[...]).astype(BF16)
    mix = jnp.concatenate([na, nb], axis=1)
    x = x_ref[...] + jnp.dot(mix, w_ref[...], preferred_element_type=F32)
    xo_ref[...] = x
    h_ref[...] = _rms(x, gf_ref[...]).astype(BF16)


def _out_proj(a, b, x, ga, gb, w_all, layer, gf):
    t = x.shape[0]
    tm = 512
    row = lambda n: pl.BlockSpec((tm, n), lambda m: (m, 0))
    const = lambda r, c: pl.BlockSpec((r, c), lambda m: (0, 0))
    return pl.pallas_call(
        _out_proj_kernel,
        grid=(t // tm,),
        in_specs=[row(D_A), row(D_B), row(D_MODEL), const(1, D_A), const(1, D_B),
                  pl.BlockSpec((None, D_MIX, D_MODEL), lambda m: (layer, 0, 0)),
                  const(1, D_MODEL)],
        out_specs=[row(D_MODEL), row(D_MODEL)],
        out_shape=[jax.ShapeDtypeStruct((t, D_MODEL), F32),
                   jax.ShapeDtypeStruct((t, D_MODEL), BF16)],
        compiler_params=pltpu.CompilerParams(
            dimension_semantics=("arbitrary",), vmem_limit_bytes=VMEM_LIMIT),
        name="out_proj",
    )(a, b, x, ga.reshape(1, D_A), gb.reshape(1, D_B), w_all, gf.reshape(1, D_MODEL))


def _ffn_kernel(h_ref, wg_ref, wu_ref, wd_ref, x_ref, gn_ref, xo_ref, hn_ref, acc_ref):
    f = pl.program_id(1)

    @pl.when(f == 0)
    def _():
        acc_ref[...] = jnp.zeros_like(acc_ref)

    h = h_ref[...]
    gate = jnp.dot(h, wg_ref[...], preferred_element_type=F32)
    up = jnp.dot(h, wu_ref[...], preferred_element_type=F32)
    act = (jax.nn.silu(gate) * up).astype(BF16)
    acc_ref[...] += jnp.dot(act, wd_ref[...], preferred_element_type=F32)

    @pl.when(f == pl.num_programs(1) - 1)
    def _():
        x = x_ref[...] + acc_ref[...]
        xo_ref[...] = x
        hn_ref[...] = _rms(x, gn_ref[...]).astype(BF16)


def _ffn(h, wg_all, wu_all, wd_all, layer, x, gn):
    t = x.shape[0]
    tm, tf = 512, 512
    return pl.pallas_call(
        _ffn_kernel,
        grid=(t // tm, D_FF // tf),
        in_specs=[pl.BlockSpec((tm, D_MODEL), lambda m, f: (m, 0)),
                  pl.BlockSpec((None, D_MODEL, tf), lambda m, f: (layer, 0, f)),
                  pl.BlockSpec((None, D_MODEL, tf), lambda m, f: (layer, 0, f)),
                  pl.BlockSpec((None, tf, D_MODEL), lambda m, f: (layer, f, 0)),
                  pl.BlockSpec((tm, D_MODEL), lambda m, f: (m, 0)),
                  pl.BlockSpec((1, D_MODEL), lambda m, f: (0, 0))],
        out_specs=[pl.BlockSpec((tm, D_MODEL), lambda m, f: (m, 0)),
                   pl.BlockSpec((tm, D_MODEL), lambda m, f: (m, 0))],
        out_shape=[jax.ShapeDtypeStruct((t, D_MODEL), F32),
                   jax.ShapeDtypeStruct((t, D_MODEL), BF16)],
        scratch_shapes=[pltpu.VMEM((tm, D_MODEL), F32)],
        compiler_params=pltpu.CompilerParams(
            dimension_semantics=("arbitrary", "arbitrary"), vmem_limit_bytes=VMEM_LIMIT),
        name="ffn",
    )(h, wg_all, wu_all, wd_all, x, gn.reshape(1, D_MODEL))


def kernel(x, attn_norm, w_in, sinks, out_norm_a, out_norm_b, w_out,
           ffn_norm, w_gate, w_up, w_down, final_norm):
    b, s_len, d = x.shape
    assert (s_len, d) == (SEQ, D_MODEL)
    depth = w_in.shape[0]

    idx = jnp.arange(N_ALIBI, dtype=F32)
    slopes = jnp.exp2(-8.0 * (idx + 1.0) / N_ALIBI)
    slopes_a, slopes_b = slopes[0::2], slopes[1::2]
    d16, d4, d1a, d1b = (jnp.asarray(p) for p in _dist_patterns())
    e1 = jnp.asarray(_ones_pattern(BLK), BF16)
    e2 = jnp.asarray(_ones_pattern(2 * BLK), BF16)

    w_in, w_out, w_gate, w_up, w_down = (
        w.astype(BF16) for w in (w_in, w_out, w_gate, w_up, w_down))

    xp, h = _norm_permute(x, attn_norm[0])
    for l in range(depth):
        *qkv_a, qb1, kvb1 = _in_proj(h, w_in, l, b)
        out_a = _attn_a(qkv_a, slopes_a, d16, d4, d1a, e1, e2).reshape(b * SEQ, D_A)
        out_b = _attn_b(qb1, kvb1, slopes_b, sinks[l], d1b, e2).reshape(b * SEQ, D_B)
        xp, h = _out_proj(out_a, out_b, xp, out_norm_a[l], out_norm_b[l], w_out, l, ffn_norm[l])
        g_next = attn_norm[l + 1] if l + 1 < depth else final_norm
        xp, h = _ffn(h, w_gate, w_up, w_down, l, xp, g_next)
    return _final_norm_unpermute(xp, final_norm, b)
```

```python
import math

import numpy as np
import jax
import jax.numpy as jnp
from jax import lax
from jax.experimental import pallas as pl
from jax.experimental.pallas import tpu as pltpu

F32 = jnp.float32
BF16 = jnp.bfloat16

D_MODEL = 2048
SEQ = 2048
HEAD_DIM = 64
N_HEADS_A = 16
N_HEADS_B = 16
N_KV_B = 2
D_A = N_HEADS_A * HEAD_DIM
D_B = N_HEADS_B * HEAD_DIM
D_KV_B = N_KV_B * HEAD_DIM
D_MIX = D_A + D_B
IN_COLS = 3 * D_A + D_B + 2 * D_KV_B
N_ALIBI = N_HEADS_A + N_HEADS_B
D_FF = 5632
EPS = 1e-6
SWA_WINDOW = 128
DIL_MAX_DIST = 128

LANES = 128
BLK = 128
N_RES = 16
N_J = SEQ // N_RES
N_BLK = SEQ // BLK
GROUP = 4
N_GROUPS = N_BLK // GROUP
LOG2E = math.log2(math.e)
VMEM_LIMIT = 56 * 1024 * 1024


def _rms(x, g):
    return x * lax.rsqrt(jnp.mean(x * x, axis=-1, keepdims=True) + EPS) * g


def _dist_patterns():
    inf = np.float32(np.inf)
    j = np.arange(BLK)
    d16 = (j[:, None] - j[None, :]).astype(np.float32)
    d16 = np.where(d16 >= 0, 16.0 * d16, inf).astype(np.float32)

    t = np.arange(BLK) // 32
    jj = np.arange(BLK) % 32
    sub = 4 * jj + t
    cur = sub[:, None] - sub[None, :]
    prev = cur + BLK
    d4 = np.concatenate([prev, cur], axis=1).astype(np.float32)
    d4 = np.where((d4 >= 0) & (d4 <= DIL_MAX_DIST), 4.0 * d4, inf).astype(np.float32)

    r = np.arange(BLK) // 8
    jj = np.arange(BLK) % 8
    pos = 16 * jj + r
    cur = pos[:, None] - pos[None, :]
    prev = cur + BLK
    d1 = np.concatenate([prev, cur], axis=1).astype(np.float32)
    d1a = np.where((d1 >= 0) & (d1 <= DIL_MAX_DIST), d1, inf).astype(np.float32)
    d1b = np.where((d1 >= 0) & (d1 <= SWA_WINDOW - 1), d1, inf).astype(np.float32)
    return d16, d4, d1a, d1b


def _ones_pattern(nk):
    e = np.zeros((2 * nk, LANES), np.float32)
    e[:nk, :HEAD_DIM] = 1.0
    e[nk:, HEAD_DIM:] = 1.0
    return e


def _to_residue_major_kernel(x_ref, o_ref):
    for r in range(N_RES):
        o_ref[r * N_J:(r + 1) * N_J, :] = x_ref[pl.ds(r, N_J, stride=N_RES), :]


def _from_residue_major_kernel(x_ref, o_ref):
    for r in range(N_RES):
        o_ref[pl.ds(r, N_J, stride=N_RES), :] = x_ref[r * N_J:(r + 1) * N_J, :]


def _permute(x, body, name):
    b = x.shape[0]
    slab = pl.BlockSpec((None, SEQ, LANES), lambda i, c: (i, 0, c))
    return pl.pallas_call(
        body,
        grid=(b, D_MODEL // LANES),
        in_specs=[slab],
        out_specs=slab,
        out_shape=jax.ShapeDtypeStruct(x.shape, F32),
        compiler_params=pltpu.CompilerParams(
            dimension_semantics=("arbitrary", "arbitrary"), vmem_limit_bytes=VMEM_LIMIT),
        name=name,
    )(x)


def _norm_kernel(x_ref, g_ref, h_ref):
    h_ref[...] = _rms(x_ref[...], g_ref[...]).astype(BF16)


def _norm(x, g):
    t = x.shape[0]
    tm = 512
    return pl.pallas_call(
        _norm_kernel,
        grid=(t // tm,),
        in_specs=[pl.BlockSpec((tm, D_MODEL), lambda m: (m, 0)),
                  pl.BlockSpec((1, D_MODEL), lambda m: (0, 0))],
        out_specs=pl.BlockSpec((tm, D_MODEL), lambda m: (m, 0)),
        out_shape=jax.ShapeDtypeStruct((t, D_MODEL), BF16),
        compiler_params=pltpu.CompilerParams(
            dimension_semantics=("arbitrary",), vmem_limit_bytes=VMEM_LIMIT),
        name="first_norm",
    )(x, g.reshape(1, D_MODEL))


IN_PROJ_RES = 2
IN_PROJ_TM = IN_PROJ_RES * N_J
IN_PROJ_STEPS = N_RES // IN_PROJ_RES
Q_SCALE = LOG2E * HEAD_DIM ** -0.5


def _in_proj_kernel(h_ref, w_ref, q16, k16l, k16h, v16l, v16h, q4, k4l, k4h, v4l, v4h,
                    q1, k1l, k1h, v1l, v1h, qb1, kvb1):
    h = h_ref[...]

    def order1(x):
        n = x.shape[1]
        pieces = [x[r * N_J + jb * 8:r * N_J + jb * 8 + 8]
                  for jb in range(N_J // 8) for r in range(IN_PROJ_RES)]
        return jnp.concatenate(pieces, axis=0).reshape(N_J // 8, 8 * IN_PROJ_RES, n)

    def proj(col, width, scale=None):
        x = jnp.dot(h, w_ref[:, col:col + width], preferred_element_type=F32)
        return x if scale is None else x * scale

    def store3(x, o16, o4, o1):
        o16[...] = x.astype(BF16)
        o4[...] = x.reshape(IN_PROJ_RES, 4, 32, D_A).astype(BF16)
        o1[...] = order1(x).astype(BF16)

    def first_head(width):
        return (lax.broadcasted_iota(jnp.int32, (IN_PROJ_TM, width), 1) & HEAD_DIM) == 0

    store3(proj(0, D_A, Q_SCALE), q16, q4, q1)
    lo = first_head(D_A)
    for i, outs in ((1, ((k16l, k4l, k1l), (k16h, k4h, k1h))),
                    (2, ((v16l, v4l, v1l), (v16h, v4h, v1h)))):
        x = proj(i * D_A, D_A)
        store3(jnp.where(lo, x, 0.0), *outs[0])
        store3(jnp.where(lo, 0.0, x), *outs[1])
    qb1[...] = order1(proj(3 * D_A, D_B, Q_SCALE)).astype(BF16)

    x = proj(3 * D_A + D_B, 2 * D_KV_B)
    lo = first_head(LANES)
    tiles = []
    for part in (x[:, :LANES], x[:, LANES:]):
        swapped = pltpu.roll(part, HEAD_DIM, 1)
        tiles += [jnp.where(lo, part, 0.0), jnp.where(lo, 0.0, swapped),
                  jnp.where(lo, swapped, 0.0), jnp.where(lo, 0.0, part)]
    kvb1[...] = order1(jnp.concatenate(tiles, axis=1)).astype(BF16)


def _in_proj(h, w_all, layer, b):
    t = h.shape[0]
    per = IN_PROJ_STEPS
    o16 = pl.BlockSpec((IN_PROJ_TM, D_A), lambda m: (m, 0))
    o4 = pl.BlockSpec((None, IN_PROJ_RES, 4, None, 32, D_A),
                      lambda m: (m // per, (m % per) % 2, 0, (m % per) // 2, 0, 0))
    o1 = lambda c: pl.BlockSpec((None, N_J // 8, None, 8 * IN_PROJ_RES, c),
                                lambda m: (m // per, 0, m % per, 0, 0))
    s16 = jax.ShapeDtypeStruct((t, D_A), BF16)
    s4 = jax.ShapeDtypeStruct((b, 4, 4, 4, 32, D_A), BF16)
    s1 = lambda c: jax.ShapeDtypeStruct((b, N_J // 8, per, 8 * IN_PROJ_RES, c), BF16)
    outs = pl.pallas_call(
        _in_proj_kernel,
        grid=(t // IN_PROJ_TM,),
        in_specs=[pl.BlockSpec((IN_PROJ_TM, D_MODEL), lambda m: (m, 0)),
                  pl.BlockSpec((None, D_MODEL, IN_COLS), lambda m: (layer, 0, 0),
                               pipeline_mode=pl.Buffered(1))],
        out_specs=[o16] * 5 + [o4] * 5 + [o1(D_A)] * 5 + [o1(D_B), o1(8 * LANES)],
        out_shape=[s16] * 5 + [s4] * 5 + [s1(D_A)] * 5 + [s1(D_B), s1(8 * LANES)],
        compiler_params=pltpu.CompilerParams(
            dimension_semantics=("arbitrary",), vmem_limit_bytes=VMEM_LIMIT),
        name="in_proj",
    )(h, w_all)
    return [o.reshape(b, SEQ, o.shape[-1]) for o in outs]


def _lane_lo():
    return lax.broadcasted_iota(jnp.int32, (BLK, LANES), 1) < HEAD_DIM


def _kv_window(pair, start, size):
    return jnp.concatenate([ref[pl.ds(start, size), :] for ref in pair], axis=0)


def _scores_softmax(qs, ks, q_starts, k_starts, nk, bias0, bias1):
    q = jnp.stack([qs[pl.ds(s, BLK), :] for s in q_starts])
    kk = jnp.stack([_kv_window(ks, s, nk) for s in k_starts])
    s = jnp.einsum("gqd,gkd->gqk", q, kk, preferred_element_type=F32)
    l0 = s[:, :, :nk] + bias0
    l1 = s[:, :, nk:] + bias1
    m0 = jnp.max(l0, axis=2, keepdims=True)
    m1 = jnp.max(l1, axis=2, keepdims=True)
    p = jnp.concatenate([jnp.exp2(l0 - m0), jnp.exp2(l1 - m1)], axis=2).astype(BF16)
    return p, m0, m1


def _weighted_values(p, vs, ones_ref, k_starts, nk):
    ones = ones_ref[...]
    vv = jnp.stack([jnp.concatenate([_kv_window(vs, s, nk), ones], axis=1) for s in k_starts])
    na = jnp.einsum("gqk,gkd->gqd", p, vv, preferred_element_type=F32)
    return na[:, :, :LANES], na[:, :, LANES:]


def _band_starts(rows, first_mode):
    if first_mode is None:
        return rows
    starts = [r - BLK for r in rows]
    if first_mode == "every" or rows[0] == 0:
        starts[0] = rows[0]
    return starts


def _pipelined(n_groups, first_half, second_half):
    first_half(0, 0)
    for g in range(n_groups - 1):
        first_half(g + 1, (g + 1) % 2)
        second_half(g, g % 2)
    second_half(n_groups - 1, (n_groups - 1) % 2)


def _first_block_bias(bias, bias_first):
    for h in range(2):
        bias_first[h, :, :BLK] = bias[h, :, BLK:]
        bias_first[h, :, BLK:] = jnp.full((BLK, BLK), -jnp.inf, F32)


def _attn_a_kernel(slopes_ref, q16, k16l, k16h, v16l, v16h, q4, k4l, k4h, v4l, v4h,
                   q1, k1l, k1h, v1l, v1h, d16_ref, d4_ref, d1_ref, e1_ref, e2_ref,
                   o_ref, b16, b4, b4f, b1, b1f, num_s, m_s, den_s, p_scr):
    hp = pl.program_id(1)
    lo = _lane_lo()
    ns0 = -LOG2E * slopes_ref[2 * hp]
    ns1 = -LOG2E * slopes_ref[2 * hp + 1]
    k16, v16, k4, v4, k1, v1 = ((k16l, k16h), (v16l, v16h), (k4l, k4h), (v4l, v4h),
                                (k1l, k1h), (v1l, v1h))

    for bias, dist in ((b16, d16_ref), (b4, d4_ref), (b1, d1_ref)):
        bias[0] = ns0 * dist[...]
        bias[1] = ns1 * dist[...]
    _first_block_bias(b4, b4f)
    _first_block_bias(b1, b1f)

    def scatter(dst_scr, branch, x, dsts, size):
        for c, d in enumerate(dsts):
            dst_scr[branch, pl.ds(d, size), :] = x[c * size:(c + 1) * size]

    def blocks_of(g):
        return [(g * GROUP + i) * BLK for i in range(GROUP)]

    def branch_halves(qs, ks, vs, ones, nk, bias, bias_first, first_mode, dsts, size, keep, emit):
        def first_of(g):
            return first_mode == "every" or (first_mode == "seq_start" and g == 0)

        def first_half(g, slot):
            rows = blocks_of(g)
            if first_of(g):
                bias0, bias1 = (jnp.stack([bias_first[h]] + [bias[h]] * (GROUP - 1))
                                for h in range(2))
            else:
                bias0, bias1 = bias[0], bias[1]
            p, m0, m1 = _scores_softmax(qs, ks, rows, _band_starts(rows, first_mode), nk,
                                        bias0, bias1)
            p_scr[slot, :, :, :2 * nk] = p
            for i in range(GROUP):
                keep(g, slot, i, jnp.where(lo, m0[i], m1[i]))

        def second_half(g, slot):
            rows = blocks_of(g)
            num, den = _weighted_values(p_scr[slot, :, :, :2 * nk], vs, ones,
                                        _band_starts(rows, first_mode), nk)
            for i in range(GROUP):
                emit(g, slot, i, num[i], den[i])

        return first_half, second_half

    def to_scratch(branch, dsts, size):
        def keep(g, slot, i, m):
            scatter(m_s, branch, m, dsts(g, i), size)

        def emit(g, slot, i, num, den):
            scatter(num_s, branch, num, dsts(g, i), size)
            scatter(den_s, branch, den, dsts(g, i), size)

        return keep, emit

    def dst16(g, i):
        return [(g * GROUP + i) * BLK]

    def dst4(r4, mb):
        return [(r4 + 4 * t) * N_J + 32 * mb for t in range(4)]

    def dst1(g, i):
        blk = g * GROUP + i
        return [r * N_J + 8 * blk for r in range(N_RES)]

    _pipelined(N_GROUPS, *branch_halves(q16, k16, v16, e1_ref, BLK, b16, None, None,
                                        dst16, BLK, *to_scratch(2, dst16, BLK)))
    _pipelined(N_GROUPS, *branch_halves(q4, k4, v4, e2_ref, 2 * BLK, b4, b4f, "every",
                                        dst4, 32, *to_scratch(1, dst4, 32)))
    _pipelined(N_GROUPS, *branch_halves(q1, k1, v1, e2_ref, 2 * BLK, b1, b1f, "seq_start",
                                        dst1, 8, *to_scratch(0, dst1, 8)))

    def merge(g, carry):
        rows = pl.ds(pl.multiple_of(g * BLK, BLK), BLK)
        m = [m_s[i, rows, :] for i in range(3)]
        m_all = jnp.maximum(jnp.maximum(m[0], m[1]), m[2])
        w = [jnp.exp2(mi - m_all) for mi in m]
        numer = num_s[0, rows, :] * w[0] + num_s[1, rows, :] * w[1] + num_s[2, rows, :] * w[2]
        denom = den_s[0, rows, :] * w[0] + den_s[1, rows, :] * w[1] + den_s[2, rows, :] * w[2]
        o_ref[rows, :] = numer / denom
        return carry

    lax.fori_loop(0, N_BLK, merge, 0, unroll=2)


def _attn_a(qkv, slopes_a, d16, d4, d1a, e1, e2):
    b = qkv[0].shape[0]
    n_pairs = N_HEADS_A // 2
    col = pl.BlockSpec((None, SEQ, LANES), lambda i, hp: (i, 0, hp))
    const = lambda shape: pl.BlockSpec(shape, lambda i, hp: (0, 0))
    band = pltpu.VMEM((2, BLK, 2 * BLK), F32)
    stat = pltpu.VMEM((3, SEQ, LANES), F32)
    return pl.pallas_call(
        _attn_a_kernel,
        grid=(b, n_pairs),
        in_specs=[pl.BlockSpec(memory_space=pltpu.SMEM)] + [col] * 15 + [
            const((BLK, BLK)), const((BLK, 2 * BLK)), const((BLK, 2 * BLK)),
            const((2 * BLK, LANES)), const((4 * BLK, LANES))],
        out_specs=pl.BlockSpec((None, SEQ, LANES), lambda i, hp: (i, 0, hp)),
        out_shape=jax.ShapeDtypeStruct((b, SEQ, D_A), F32),
        scratch_shapes=[pltpu.VMEM((2, BLK, BLK), F32), band, band, band, band,
                        stat, stat, stat, pltpu.VMEM((2, GROUP, BLK, 4 * BLK), BF16)],
        compiler_params=pltpu.CompilerParams(
            dimension_semantics=("arbitrary", "arbitrary"), vmem_limit_bytes=VMEM_LIMIT),
        name="attn_dilated",
    )(slopes_a, *qkv, d16, d4, d1a, e1, e2)


def _attn_b_kernel(slopes_ref, sinks_ref, q1, k1l, k1h, v1l, v1h, d1_ref, e2_ref, o_ref,
                   b1, b1f, p_scr, m_scr):
    hp = pl.program_id(1)
    lo = _lane_lo()
    sink0 = LOG2E * sinks_ref[2 * hp]
    sink1 = LOG2E * sinks_ref[2 * hp + 1]
    b1[0] = (-LOG2E * slopes_ref[2 * hp]) * d1_ref[...]
    b1[1] = (-LOG2E * slopes_ref[2 * hp + 1]) * d1_ref[...]
    _first_block_bias(b1, b1f)
    k1, v1 = (k1l, k1h), (v1l, v1h)

    sink = jnp.where(lo, sink0, sink1)

    def blocks_of(g):
        return [(g * GROUP + i) * BLK for i in range(GROUP)]

    def first_half(g, slot):
        rows = blocks_of(g)
        if g == 0:
            bias0, bias1 = (jnp.stack([b1f[h]] + [b1[h]] * (GROUP - 1)) for h in range(2))
        else:
            bias0, bias1 = b1[0], b1[1]
        p, m0, m1 = _scores_softmax(q1, k1, rows, _band_starts(rows, "seq_start"), 2 * BLK,
                                    bias0, bias1)
        p_scr[slot] = p
        for i in range(GROUP):
            m_scr[slot, i] = jnp.where(lo, m0[i], m1[i])

    def second_half(g, slot):
        rows = blocks_of(g)
        num, den = _weighted_values(p_scr[slot], v1, e2_ref,
                                    _band_starts(rows, "seq_start"), 2 * BLK)
        for i in range(GROUP):
            m = m_scr[slot, i]
            m2 = jnp.maximum(m, sink)
            c = jnp.exp2(m - m2)
            out = (num[i] * c) / (den[i] * c + jnp.exp2(sink - m2))
            blk = g * GROUP + i
            for r in range(N_RES):
                o_ref[pl.ds(r * N_J + 8 * blk, 8), :] = out[r * 8:(r + 1) * 8]

    _pipelined(N_GROUPS, first_half, second_half)


def _attn_b(q1, kv1, slopes_b, sinks, d1b, e2):
    b = q1.shape[0]
    n_pairs = N_HEADS_B // 2
    pairs_per_kv = n_pairs // N_KV_B
    kv_tile = lambda tile: pl.BlockSpec(
        (None, SEQ, LANES), lambda i, hp, tile=tile: (i, 0, tile + 2 * (hp // pairs_per_kv)))
    band = pltpu.VMEM((2, BLK, 2 * BLK), F32)
    return pl.pallas_call(
        _attn_b_kernel,
        grid=(b, n_pairs),
        in_specs=[pl.BlockSpec(memory_space=pltpu.SMEM),
                  pl.BlockSpec(memory_space=pltpu.SMEM),
                  pl.BlockSpec((None, SEQ, LANES), lambda i, hp: (i, 0, hp)),
                  kv_tile(0), kv_tile(1), kv_tile(4), kv_tile(5),
                  pl.BlockSpec((BLK, 2 * BLK), lambda i, hp: (0, 0)),
                  pl.BlockSpec((4 * BLK, LANES), lambda i, hp: (0, 0))],
        out_specs=pl.BlockSpec((None, SEQ, LANES), lambda i, hp: (i, 0, hp)),
        out_shape=jax.ShapeDtypeStruct((b, SEQ, D_B), F32),
        scratch_shapes=[band, band, pltpu.VMEM((2, GROUP, BLK, 4 * BLK), BF16),
                        pltpu.VMEM((2, GROUP, BLK, LANES), F32)],
        compiler_params=pltpu.CompilerParams(
            dimension_semantics=("arbitrary", "arbitrary"), vmem_limit_bytes=VMEM_LIMIT),
        name="attn_swa",
    )(slopes_b, sinks, q1, kv1, kv1, kv1, kv1, d1b, e2)


def _out_proj_kernel(a_ref, b_ref, x_ref, ga_ref, gb_ref, w_ref, gf_ref, xo_ref, h_ref):
    na = _rms(a_ref[...], ga_ref[...]).astype(BF16)
    nb = _rms(b_ref[...], gb_ref[...]).astype(BF16)
    mix = jnp.concatenate([na, nb], axis=1)
    x = x_ref[...] + jnp.dot(mix, w_ref[...], preferred_element_type=F32)
    xo_ref[...] = x
    h_ref[...] = _rms(x, gf_ref[...]).astype(BF16)


def _out_proj(a, b, x, ga, gb, w_all, layer, gf):
    t = x.shape[0]
    tm = 512
    row = lambda n: pl.BlockSpec((tm, n), lambda m: (m, 0))
    const = lambda r, c: pl.BlockSpec((r, c), lambda m: (0, 0))
    return pl.pallas_call(
        _out_proj_kernel,
        grid=(t // tm,),
        in_specs=[row(D_A), row(D_B), row(D_MODEL), const(1, D_A), const(1, D_B),
                  pl.BlockSpec((None, D_MIX, D_MODEL), lambda m: (layer, 0, 0)),
                  const(1, D_MODEL)],
        out_specs=[row(D_MODEL), row(D_MODEL)],
        out_shape=[jax.ShapeDtypeStruct((t, D_MODEL), F32),
                   jax.ShapeDtypeStruct((t, D_MODEL), BF16)],
        compiler_params=pltpu.CompilerParams(
            dimension_semantics=("arbitrary",), vmem_limit_bytes=VMEM_LIMIT),
        name="out_proj",
    )(a, b, x, ga.reshape(1, D_A), gb.reshape(1, D_B), w_all, gf.reshape(1, D_MODEL))


def _ffn_kernel(h_ref, wg_ref, wu_ref, wd_ref, x_ref, gn_ref, xo_ref, hn_ref, acc_ref):
    f = pl.program_id(1)

    @pl.when(f == 0)
    def _():
        acc_ref[...] = jnp.zeros_like(acc_ref)

    h = h_ref[...]
    gate = jnp.dot(h, wg_ref[...], preferred_element_type=F32)
    up = jnp.dot(h, wu_ref[...], preferred_element_type=F32)
    act = (jax.nn.silu(gate) * up).astype(BF16)
    acc_ref[...] += jnp.dot(act, wd_ref[...], preferred_element_type=F32)

    @pl.when(f == pl.num_programs(1) - 1)
    def _():
        x = x_ref[...] + acc_ref[...]
        xo_ref[...] = x
        hn_ref[...] = _rms(x, gn_ref[...]).astype(hn_ref.dtype)


def _ffn(h, wg_all, wu_all, wd_all, layer, x, gn, normed_dtype):
    t = x.shape[0]
    tm, tf = 512, 512
    return pl.pallas_call(
        _ffn_kernel,
        grid=(t // tm, D_FF // tf),
        in_specs=[pl.BlockSpec((tm, D_MODEL), lambda m, f: (m, 0)),
                  pl.BlockSpec((None, D_MODEL, tf), lambda m, f: (layer, 0, f)),
                  pl.BlockSpec((None, D_MODEL, tf), lambda m, f: (layer, 0, f)),
                  pl.BlockSpec((None, tf, D_MODEL), lambda m, f: (layer, f, 0)),
                  pl.BlockSpec((tm, D_MODEL), lambda m, f: (m, 0)),
                  pl.BlockSpec((1, D_MODEL), lambda m, f: (0, 0))],
        out_specs=[pl.BlockSpec((tm, D_MODEL), lambda m, f: (m, 0)),
                   pl.BlockSpec((tm, D_MODEL), lambda m, f: (m, 0))],
        out_shape=[jax.ShapeDtypeStruct((t, D_MODEL), F32),
                   jax.ShapeDtypeStruct((t, D_MODEL), normed_dtype)],
        scratch_shapes=[pltpu.VMEM((tm, D_MODEL), F32)],
        compiler_params=pltpu.CompilerParams(
            dimension_semantics=("arbitrary", "arbitrary"), vmem_limit_bytes=VMEM_LIMIT),
        name="ffn",
    )(h, wg_all, wu_all, wd_all, x, gn.reshape(1, D_MODEL))


def kernel(x, attn_norm, w_in, sinks, out_norm_a, out_norm_b, w_out,
           ffn_norm, w_gate, w_up, w_down, final_norm):
    b, s_len, d = x.shape
    assert (s_len, d) == (SEQ, D_MODEL)
    depth = w_in.shape[0]

    idx = jnp.arange(N_ALIBI, dtype=F32)
    slopes = jnp.exp2(-8.0 * (idx + 1.0) / N_ALIBI)
    slopes_a, slopes_b = slopes[0::2], slopes[1::2]
    d16, d4, d1a, d1b = (jnp.asarray(p) for p in _dist_patterns())
    e1 = jnp.asarray(_ones_pattern(BLK), BF16)
    e2 = jnp.asarray(_ones_pattern(2 * BLK), BF16)

    w_in, w_out, w_gate, w_up, w_down = (
        w.astype(BF16) for w in (w_in, w_out, w_gate, w_up, w_down))

    xp = _permute(x, _to_residue_major_kernel, "to_residue_major").reshape(b * SEQ, D_MODEL)
    h = _norm(xp, attn_norm[0])
    for l in range(depth):
        last = l + 1 == depth
        *qkv_a, qb1, kvb1 = _in_proj(h, w_in, l, b)
        out_a = _attn_a(qkv_a, slopes_a, d16, d4, d1a, e1, e2).reshape(b * SEQ, D_A)
        out_b = _attn_b(qb1, kvb1, slopes_b, sinks[l], d1b, e2).reshape(b * SEQ, D_B)
        xp, h = _out_proj(out_a, out_b, xp, out_norm_a[l], out_norm_b[l], w_out, l, ffn_norm[l])
        xp, h = _ffn(h, w_gate, w_up, w_down, l, xp,
                     final_norm if last else attn_norm[l + 1], F32 if last else BF16)
    return _permute(h.reshape(b, SEQ, D_MODEL), _from_residue_major_kernel, "from_residue_major")
```

```python
import math

import numpy as np
import jax
import jax.numpy as jnp
from jax import lax
from jax.experimental import pallas as pl
from jax.experimental.pallas import tpu as pltpu

F32 = jnp.float32
BF16 = jnp.bfloat16

D_MODEL = 2048
SEQ = 2048
HEAD_DIM = 64
N_HEADS_A = 16
N_HEADS_B = 16
N_KV_B = 2
D_A = N_HEADS_A * HEAD_DIM
D_B = N_HEADS_B * HEAD_DIM
D_KV_B = N_KV_B * HEAD_DIM
D_MIX = D_A + D_B
IN_COLS = 3 * D_A + D_B + 2 * D_KV_B
N_ALIBI = N_HEADS_A + N_HEADS_B
D_FF = 5632
EPS = 1e-6
SWA_WINDOW = 128
DIL_MAX_DIST = 128

LANES = 128
BLK = 128
N_RES = 16
N_J = SEQ // N_RES
N_BLK = SEQ // BLK
GROUP = 4
N_GROUPS = N_BLK // GROUP
LOG2E = math.log2(math.e)
VMEM_LIMIT = 56 * 1024 * 1024


def _rms(x, g):
    return x * lax.rsqrt(jnp.mean(x * x, axis=-1, keepdims=True) + EPS) * g


def _dist_patterns():
    inf = np.float32(np.inf)
    j = np.arange(BLK)
    d16 = (j[:, None] - j[None, :]).astype(np.float32)
    d16 = np.where(d16 >= 0, 16.0 * d16, inf).astype(np.float32)

    t = np.arange(BLK) // 32
    jj = np.arange(BLK) % 32
    sub = 4 * jj + t
    cur = sub[:, None] - sub[None, :]
    prev = cur + BLK
    d4 = np.concatenate([prev, cur], axis=1).astype(np.float32)
    d4 = np.where((d4 >= 0) & (d4 <= DIL_MAX_DIST), 4.0 * d4, inf).astype(np.float32)

    r = np.arange(BLK) // 8
    jj = np.arange(BLK) % 8
    pos = 16 * jj + r
    cur = pos[:, None] - pos[None, :]
    prev = cur + BLK
    d1 = np.concatenate([prev, cur], axis=1).astype(np.float32)
    d1a = np.where((d1 >= 0) & (d1 <= DIL_MAX_DIST), d1, inf).astype(np.float32)
    d1b = np.where((d1 >= 0) & (d1 <= SWA_WINDOW - 1), d1, inf).astype(np.float32)
    return d16, d4, d1a, d1b


def _ones_pattern(nk):
    e = np.zeros((2 * nk, LANES), np.float32)
    e[:nk, :HEAD_DIM] = 1.0
    e[nk:, HEAD_DIM:] = 1.0
    return e


def _to_residue_major_kernel(x_ref, o_ref):
    for r in range(N_RES):
        o_ref[r * N_J:(r + 1) * N_J, :] = x_ref[pl.ds(r, N_J, stride=N_RES), :]


def _from_residue_major_kernel(x_ref, o_ref):
    for r in range(N_RES):
        o_ref[pl.ds(r, N_J, stride=N_RES), :] = x_ref[r * N_J:(r + 1) * N_J, :]


def _permute(x, body, name):
    b = x.shape[0]
    slab = pl.BlockSpec((None, SEQ, LANES), lambda i, c: (i, 0, c))
    return pl.pallas_call(
        body,
        grid=(b, D_MODEL // LANES),
        in_specs=[slab],
        out_specs=slab,
        out_shape=jax.ShapeDtypeStruct(x.shape, F32),
        compiler_params=pltpu.CompilerParams(
            dimension_semantics=("arbitrary", "arbitrary"), vmem_limit_bytes=VMEM_LIMIT),
        name=name,
    )(x)


def _norm_kernel(x_ref, g_ref, h_ref):
    h_ref[...] = _rms(x_ref[...], g_ref[...]).astype(BF16)


def _norm(x, g):
    t = x.shape[0]
    tm = 512
    return pl.pallas_call(
        _norm_kernel,
        grid=(t // tm,),
        in_specs=[pl.BlockSpec((tm, D_MODEL), lambda m: (m, 0)),
                  pl.BlockSpec((1, D_MODEL), lambda m: (0, 0))],
        out_specs=pl.BlockSpec((tm, D_MODEL), lambda m: (m, 0)),
        out_shape=jax.ShapeDtypeStruct((t, D_MODEL), BF16),
        compiler_params=pltpu.CompilerParams(
            dimension_semantics=("arbitrary",), vmem_limit_bytes=VMEM_LIMIT),
        name="first_norm",
    )(x, g.reshape(1, D_MODEL))


IN_PROJ_RES = 2
IN_PROJ_TM = IN_PROJ_RES * N_J
IN_PROJ_STEPS = N_RES // IN_PROJ_RES
Q_SCALE = LOG2E * HEAD_DIM ** -0.5


def _in_proj_kernel(h_ref, w_ref, q16, k16l, k16h, v16l, v16h, q4, k4l, k4h, v4l, v4h,
                    q1, k1l, k1h, v1l, v1h, qb1, kvb1):
    h = h_ref[...]

    def order1(x):
        n = x.shape[1]
        pieces = [x[r * N_J + jb * 8:r * N_J + jb * 8 + 8]
                  for jb in range(N_J // 8) for r in range(IN_PROJ_RES)]
        return jnp.concatenate(pieces, axis=0).reshape(N_J // 8, 8 * IN_PROJ_RES, n)

    def proj(col, width, scale=None):
        x = jnp.dot(h, w_ref[:, col:col + width], preferred_element_type=F32)
        return x if scale is None else x * scale

    def store3(x, o16, o4, o1):
        o16[...] = x.astype(BF16)
        o4[...] = x.reshape(IN_PROJ_RES, 4, 32, D_A).astype(BF16)
        o1[...] = order1(x).astype(BF16)

    def first_head(width):
        return (lax.broadcasted_iota(jnp.int32, (IN_PROJ_TM, width), 1) & HEAD_DIM) == 0

    store3(proj(0, D_A, Q_SCALE), q16, q4, q1)
    lo = first_head(D_A)
    for i, outs in ((1, ((k16l, k4l, k1l), (k16h, k4h, k1h))),
                    (2, ((v16l, v4l, v1l), (v16h, v4h, v1h)))):
        x = proj(i * D_A, D_A)
        store3(jnp.where(lo, x, 0.0), *outs[0])
        store3(jnp.where(lo, 0.0, x), *outs[1])
    qb1[...] = order1(proj(3 * D_A, D_B, Q_SCALE)).astype(BF16)

    x = proj(3 * D_A + D_B, 2 * D_KV_B)
    lo = first_head(LANES)
    tiles = []
    for part in (x[:, :LANES], x[:, LANES:]):
        swapped = pltpu.roll(part, HEAD_DIM, 1)
        tiles += [jnp.where(lo, part, 0.0), jnp.where(lo, 0.0, swapped),
                  jnp.where(lo, swapped, 0.0), jnp.where(lo, 0.0, part)]
    kvb1[...] = order1(jnp.concatenate(tiles, axis=1)).astype(BF16)


def _in_proj(h, w, b):
    t = h.shape[0]
    per = IN_PROJ_STEPS
    o16 = pl.BlockSpec((IN_PROJ_TM, D_A), lambda m: (m, 0))
    o4 = pl.BlockSpec((None, IN_PROJ_RES, 4, None, 32, D_A),
                      lambda m: (m // per, (m % per) % 2, 0, (m % per) // 2, 0, 0))
    o1 = lambda c: pl.BlockSpec((None, N_J // 8, None, 8 * IN_PROJ_RES, c),
                                lambda m: (m // per, 0, m % per, 0, 0))
    s16 = jax.ShapeDtypeStruct((t, D_A), BF16)
    s4 = jax.ShapeDtypeStruct((b, 4, 4, 4, 32, D_A), BF16)
    s1 = lambda c: jax.ShapeDtypeStruct((b, N_J // 8, per, 8 * IN_PROJ_RES, c), BF16)
    outs = pl.pallas_call(
        _in_proj_kernel,
        grid=(t // IN_PROJ_TM,),
        in_specs=[pl.BlockSpec((IN_PROJ_TM, D_MODEL), lambda m: (m, 0)),
                  pl.BlockSpec((D_MODEL, IN_COLS), lambda m: (0, 0),
                               pipeline_mode=pl.Buffered(1))],
        out_specs=[o16] * 5 + [o4] * 5 + [o1(D_A)] * 5 + [o1(D_B), o1(8 * LANES)],
        out_shape=[s16] * 5 + [s4] * 5 + [s1(D_A)] * 5 + [s1(D_B), s1(8 * LANES)],
        compiler_params=pltpu.CompilerParams(
            dimension_semantics=("arbitrary",), vmem_limit_bytes=VMEM_LIMIT),
        name="in_proj",
    )(h, w)
    return [o.reshape(b, SEQ, o.shape[-1]) for o in outs]


def _lane_lo():
    return lax.broadcasted_iota(jnp.int32, (BLK, LANES), 1) < HEAD_DIM


def _kv_window(pair, start, size):
    return jnp.concatenate([ref[pl.ds(start, size), :] for ref in pair], axis=0)


def _scores_softmax(qs, ks, q_starts, k_starts, nk, bias0, bias1):
    q = jnp.stack([qs[pl.ds(s, BLK), :] for s in q_starts])
    kk = jnp.stack([_kv_window(ks, s, nk) for s in k_starts])
    s = jnp.einsum("gqd,gkd->gqk", q, kk, preferred_element_type=F32)
    l0 = s[:, :, :nk] + bias0
    l1 = s[:, :, nk:] + bias1
    m0 = jnp.max(l0, axis=2, keepdims=True)
    m1 = jnp.max(l1, axis=2, keepdims=True)
    p = jnp.concatenate([jnp.exp2(l0 - m0), jnp.exp2(l1 - m1)], axis=2).astype(BF16)
    return p, m0, m1


def _weighted_values(p, vs, ones_ref, k_starts, nk):
    ones = ones_ref[...]
    vv = jnp.stack([jnp.concatenate([_kv_window(vs, s, nk), ones], axis=1) for s in k_starts])
    na = jnp.einsum("gqk,gkd->gqd", p, vv, preferred_element_type=F32)
    return na[:, :, :LANES], na[:, :, LANES:]


def _is_first(row, period):
    return period is not None and row % period == 0


def _band_starts(rows, period):
    if period is None:
        return rows
    return [r if _is_first(r, period) else r - BLK for r in rows]


def _band_bias(rows, period, bias, bias_first):
    if not any(_is_first(r, period) for r in rows):
        return bias[0], bias[1]
    return tuple(jnp.stack([(bias_first if _is_first(r, period) else bias)[h] for r in rows])
                 for h in range(2))


def _pipelined(n_groups, first_half, second_half):
    first_half(0, 0)
    for g in range(n_groups - 1):
        first_half(g + 1, (g + 1) % 2)
        second_half(g, g % 2)
    second_half(n_groups - 1, (n_groups - 1) % 2)


def _first_block_bias(bias, bias_first):
    for h in range(2):
        bias_first[h, :, :BLK] = bias[h, :, BLK:]
        bias_first[h, :, BLK:] = jnp.full((BLK, BLK), -jnp.inf, F32)


def _attn_a_kernel(slopes_ref, q16, k16l, k16h, v16l, v16h, q4, k4l, k4h, v4l, v4h,
                   q1, k1l, k1h, v1l, v1h, d16_ref, d4_ref, d1_ref, e1_ref, e2_ref,
                   o_ref, b16, b4, b4f, b1, b1f, num_s, m_s, den_s, p_scr):
    hp = pl.program_id(1)
    lo = _lane_lo()
    ns0 = -LOG2E * slopes_ref[2 * hp]
    ns1 = -LOG2E * slopes_ref[2 * hp + 1]
    k16, v16, k4, v4, k1, v1 = ((k16l, k16h), (v16l, v16h), (k4l, k4h), (v4l, v4h),
                                (k1l, k1h), (v1l, v1h))

    for bias, dist in ((b16, d16_ref), (b4, d4_ref), (b1, d1_ref)):
        bias[0] = ns0 * dist[...]
        bias[1] = ns1 * dist[...]
    _first_block_bias(b4, b4f)
    _first_block_bias(b1, b1f)

    def scatter(dst_scr, branch, x, dsts, size):
        for c, d in enumerate(dsts):
            dst_scr[branch, pl.ds(d, size), :] = x[c * size:(c + 1) * size]

    def blocks_of(g):
        return [(g * GROUP + i) * BLK for i in range(GROUP)]

    def branch_halves(qs, ks, vs, ones, nk, bias, bias_first, period, dsts, size, keep, emit):
        def first_half(g, slot):
            rows = blocks_of(g)
            bias0, bias1 = _band_bias(rows, period, bias, bias_first)
            p, m0, m1 = _scores_softmax(qs, ks, rows, _band_starts(rows, period), nk,
                                        bias0, bias1)
            p_scr[slot, :, :, :2 * nk] = p
            for i in range(GROUP):
                keep(g, slot, i, jnp.where(lo, m0[i], m1[i]))

        def second_half(g, slot):
            rows = blocks_of(g)
            num, den = _weighted_values(p_scr[slot, :, :, :2 * nk], vs, ones,
                                        _band_starts(rows, period), nk)
            for i in range(GROUP):
                emit(g, slot, i, num[i], den[i])

        return first_half, second_half

    def to_scratch(branch, dsts, size):
        def keep(g, slot, i, m):
            scatter(m_s, branch, m, dsts(g, i), size)

        def emit(g, slot, i, num, den):
            scatter(num_s, branch, num, dsts(g, i), size)
            scatter(den_s, branch, den, dsts(g, i), size)

        return keep, emit

    def dst16(g, i):
        return [(g * GROUP + i) * BLK]

    def dst4(g, i):
        r4, mb = divmod(g * GROUP + i, 4)
        return [(r4 + 4 * t) * N_J + 32 * mb for t in range(4)]

    def dst1(g, i):
        blk = g * GROUP + i
        return [r * N_J + 8 * blk for r in range(N_RES)]

    _pipelined(N_GROUPS, *branch_halves(q16, k16, v16, e1_ref, BLK, b16, None, None,
                                        dst16, BLK, *to_scratch(2, dst16, BLK)))
    _pipelined(N_GROUPS, *branch_halves(q4, k4, v4, e2_ref, 2 * BLK, b4, b4f, SEQ // 4,
                                        dst4, 32, *to_scratch(1, dst4, 32)))
    _pipelined(N_GROUPS, *branch_halves(q1, k1, v1, e2_ref, 2 * BLK, b1, b1f, SEQ,
                                        dst1, 8, *to_scratch(0, dst1, 8)))

    def merge(g, carry):
        rows = pl.ds(pl.multiple_of(g * BLK, BLK), BLK)
        m = [m_s[i, rows, :] for i in range(3)]
        m_all = jnp.maximum(jnp.maximum(m[0], m[1]), m[2])
        w = [jnp.exp2(mi - m_all) for mi in m]
        numer = num_s[0, rows, :] * w[0] + num_s[1, rows, :] * w[1] + num_s[2, rows, :] * w[2]
        denom = den_s[0, rows, :] * w[0] + den_s[1, rows, :] * w[1] + den_s[2, rows, :] * w[2]
        o_ref[rows, :] = numer / denom
        return carry

    lax.fori_loop(0, N_BLK, merge, 0, unroll=2)


def _attn_a(qkv, slopes_a, d16, d4, d1a, e1, e2):
    b = qkv[0].shape[0]
    n_pairs = N_HEADS_A // 2
    col = pl.BlockSpec((None, SEQ, LANES), lambda i, hp: (i, 0, hp))
    const = lambda shape: pl.BlockSpec(shape, lambda i, hp: (0, 0))
    band = pltpu.VMEM((2, BLK, 2 * BLK), F32)
    stat = pltpu.VMEM((3, SEQ, LANES), F32)
    return pl.pallas_call(
        _attn_a_kernel,
        grid=(b, n_pairs),
        in_specs=[pl.BlockSpec(memory_space=pltpu.SMEM)] + [col] * 15 + [
            const((BLK, BLK)), const((BLK, 2 * BLK)), const((BLK, 2 * BLK)),
            const((2 * BLK, LANES)), const((4 * BLK, LANES))],
        out_specs=pl.BlockSpec((None, SEQ, LANES), lambda i, hp: (i, 0, hp)),
        out_shape=jax.ShapeDtypeStruct((b, SEQ, D_A), F32),
        scratch_shapes=[pltpu.VMEM((2, BLK, BLK), F32), band, band, band, band,
                        stat, stat, stat, pltpu.VMEM((2, GROUP, BLK, 4 * BLK), BF16)],
        compiler_params=pltpu.CompilerParams(
            dimension_semantics=("arbitrary", "arbitrary"), vmem_limit_bytes=VMEM_LIMIT),
        name="attn_dilated",
    )(slopes_a, *qkv, d16, d4, d1a, e1, e2)


def _attn_b_kernel(slopes_ref, sinks_ref, q1, k1l, k1h, v1l, v1h, d1_ref, e2_ref, o_ref,
                   b1, b1f, p_scr, m_scr):
    hp = pl.program_id(1)
    lo = _lane_lo()
    sink0 = LOG2E * sinks_ref[2 * hp]
    sink1 = LOG2E * sinks_ref[2 * hp + 1]
    b1[0] = (-LOG2E * slopes_ref[2 * hp]) * d1_ref[...]
    b1[1] = (-LOG2E * slopes_ref[2 * hp + 1]) * d1_ref[...]
    _first_block_bias(b1, b1f)
    k1, v1 = (k1l, k1h), (v1l, v1h)

    sink = jnp.where(lo, sink0, sink1)

    def blocks_of(g):
        return [(g * GROUP + i) * BLK for i in range(GROUP)]

    def first_half(g, slot):
        rows = blocks_of(g)
        bias0, bias1 = _band_bias(rows, SEQ, b1, b1f)
        p, m0, m1 = _scores_softmax(q1, k1, rows, _band_starts(rows, SEQ), 2 * BLK,
                                    bias0, bias1)
        p_scr[slot] = p
        for i in range(GROUP):
            m_scr[slot, i] = jnp.where(lo, m0[i], m1[i])

    def second_half(g, slot):
        rows = blocks_of(g)
        num, den = _weighted_values(p_scr[slot], v1, e2_ref,
                                    _band_starts(rows, SEQ), 2 * BLK)
        for i in range(GROUP):
            m = m_scr[slot, i]
            m2 = jnp.maximum(m, sink)
            c = jnp.exp2(m - m2)
            out = (num[i] * c) / (den[i] * c + jnp.exp2(sink - m2))
            blk = g * GROUP + i
            for r in range(N_RES):
                o_ref[pl.ds(r * N_J + 8 * blk, 8), :] = out[r * 8:(r + 1) * 8]

    _pipelined(N_GROUPS, first_half, second_half)


def _attn_b(q1, kv1, slopes_b, sinks, d1b, e2):
    b = q1.shape[0]
    n_pairs = N_HEADS_B // 2
    pairs_per_kv = n_pairs // N_KV_B
    kv_tile = lambda tile: pl.BlockSpec(
        (None, SEQ, LANES), lambda i, hp, tile=tile: (i, 0, tile + 2 * (hp // pairs_per_kv)))
    band = pltpu.VMEM((2, BLK, 2 * BLK), F32)
    return pl.pallas_call(
        _attn_b_kernel,
        grid=(b, n_pairs),
        in_specs=[pl.BlockSpec(memory_space=pltpu.SMEM),
                  pl.BlockSpec(memory_space=pltpu.SMEM),
                  pl.BlockSpec((None, SEQ, LANES), lambda i, hp: (i, 0, hp)),
                  kv_tile(0), kv_tile(1), kv_tile(4), kv_tile(5),
                  pl.BlockSpec((BLK, 2 * BLK), lambda i, hp: (0, 0)),
                  pl.BlockSpec((4 * BLK, LANES), lambda i, hp: (0, 0))],
        out_specs=pl.BlockSpec((None, SEQ, LANES), lambda i, hp: (i, 0, hp)),
        out_shape=jax.ShapeDtypeStruct((b, SEQ, D_B), F32),
        scratch_shapes=[band, band, pltpu.VMEM((2, GROUP, BLK, 4 * BLK), BF16),
                        pltpu.VMEM((2, GROUP, BLK, LANES), F32)],
        compiler_params=pltpu.CompilerParams(
            dimension_semantics=("arbitrary", "arbitrary"), vmem_limit_bytes=VMEM_LIMIT),
        name="attn_swa",
    )(slopes_b, sinks, q1, kv1, kv1, kv1, kv1, d1b, e2)


def _cast_chunks(src_refs, dst_refs):
    for src, dst in zip(src_refs, dst_refs):
        dst[...] = src[...].astype(BF16)


def _out_proj_kernel(a_ref, b_ref, x_ref, ga_ref, gb_ref, w_ref, gf_ref, *rest):
    n_cast = (len(rest) - 2) // 2
    cast_src, (xo_ref, h_ref), cast_dst = rest[:n_cast], rest[n_cast:n_cast + 2], rest[n_cast + 2:]
    na = _rms(a_ref[...], ga_ref[...]).astype(BF16)
    nb = _rms(b_ref[...], gb_ref[...]).astype(BF16)
    mix = jnp.concatenate([na, nb], axis=1)
    x = x_ref[...] + jnp.dot(mix, w_ref[...], preferred_element_type=F32)
    xo_ref[...] = x
    h_ref[...] = _rms(x, gf_ref[...]).astype(BF16)
    _cast_chunks(cast_src, cast_dst)


def _out_proj(a, b, x, ga, gb, w, gf, nxt):
    t = x.shape[0]
    tm = 512
    steps = t // tm
    row = lambda n: pl.BlockSpec((tm, n), lambda m: (m, 0))
    const = lambda r, c: pl.BlockSpec((r, c), lambda m: (0, 0))
    in_specs = [row(D_A), row(D_B), row(D_MODEL), const(1, D_A), const(1, D_B),
                pl.BlockSpec((D_MIX, D_MODEL), lambda m: (0, 0), pipeline_mode=pl.Buffered(1)),
                const(1, D_MODEL)]
    out_specs = [row(D_MODEL), row(D_MODEL)]
    out_shape = [jax.ShapeDtypeStruct((t, D_MODEL), F32), jax.ShapeDtypeStruct((t, D_MODEL), BF16)]
    args = [a, b, x, ga.reshape(1, D_A), gb.reshape(1, D_B), w, gf.reshape(1, D_MODEL)]
    if nxt is not None:
        *stacked, layer = nxt
        for wf in stacked:
            rows, cols = wf.shape[1] // steps, wf.shape[2]
            in_specs.append(pl.BlockSpec((None, rows, cols), lambda m: (layer, m, 0)))
            out_specs.append(pl.BlockSpec((rows, cols), lambda m: (m, 0)))
            out_shape.append(jax.ShapeDtypeStruct(wf.shape[1:], BF16))
        args += stacked
    return pl.pallas_call(
        _out_proj_kernel,
        grid=(steps,),
        in_specs=in_specs,
        out_specs=out_specs,
        out_shape=out_shape,
        compiler_params=pltpu.CompilerParams(
            dimension_semantics=("arbitrary",), vmem_limit_bytes=VMEM_LIMIT),
        name="out_proj",
    )(*args)


def _ffn_kernel(h_ref, wg_ref, wu_ref, wd_ref, x_ref, gn_ref, *rest):
    n_cast = (len(rest) - 3) // 2
    cast_src, (xo_ref, hn_ref) = rest[:n_cast], rest[n_cast:n_cast + 2]
    cast_dst, acc_ref = rest[n_cast + 2:-1], rest[-1]
    f = pl.program_id(1)

    @pl.when(f == 0)
    def _():
        acc_ref[...] = jnp.zeros_like(acc_ref)

    h = h_ref[...]
    gate = jnp.dot(h, wg_ref[...], preferred_element_type=F32)
    up = jnp.dot(h, wu_ref[...], preferred_element_type=F32)
    act = (jax.nn.silu(gate) * up).astype(BF16)
    acc_ref[...] += jnp.dot(act, wd_ref[...], preferred_element_type=F32)
    _cast_chunks(cast_src, cast_dst)

    @pl.when(f == pl.num_programs(1) - 1)
    def _():
        x = x_ref[...] + acc_ref[...]
        xo_ref[...] = x
        hn_ref[...] = _rms(x, gn_ref[...]).astype(hn_ref.dtype)


def _ffn(h, wg, wu, wd, x, gn, normed_dtype, nxt):
    t = x.shape[0]
    tm, tf = 512, 512
    n_m, n_f = t // tm, D_FF // tf
    in_specs = [pl.BlockSpec((tm, D_MODEL), lambda m, f: (m, 0)),
                pl.BlockSpec((D_MODEL, tf), lambda m, f: (0, f)),
                pl.BlockSpec((D_MODEL, tf), lambda m, f: (0, f)),
                pl.BlockSpec((tf, D_MODEL), lambda m, f: (f, 0)),
                pl.BlockSpec((tm, D_MODEL), lambda m, f: (m, 0)),
                pl.BlockSpec((1, D_MODEL), lambda m, f: (0, 0))]
    out_specs = [pl.BlockSpec((tm, D_MODEL), lambda m, f: (m, 0)),
                 pl.BlockSpec((tm, D_MODEL), lambda m, f: (m, 0))]
    out_shape = [jax.ShapeDtypeStruct((t, D_MODEL), F32),
                 jax.ShapeDtypeStruct((t, D_MODEL), normed_dtype)]
    args = [h, wg, wu, wd, x, gn.reshape(1, D_MODEL)]
    if nxt is not None:
        w_gate, w_up, w_down, layer = nxt
        up_chunk = (D_MODEL // n_m, D_FF // n_f)
        down_chunk = (D_FF // n_f, D_MODEL // n_m)
        in_specs += [pl.BlockSpec((None,) + up_chunk, lambda m, f: (layer, m, f)),
                     pl.BlockSpec((None,) + up_chunk, lambda m, f: (layer, m, f)),
                     pl.BlockSpec((None,) + down_chunk, lambda m, f: (layer, f, m))]
        out_specs += [pl.BlockSpec(up_chunk, lambda m, f: (m, f)),
                      pl.BlockSpec(up_chunk, lambda m, f: (m, f)),
                      pl.BlockSpec(down_chunk, lambda m, f: (f, m))]
        out_shape += [jax.ShapeDtypeStruct(w.shape[1:], BF16) for w in (w_gate, w_up, w_down)]
        args += [w_gate, w_up, w_down]
    return pl.pallas_call(
        _ffn_kernel,
        grid=(n_m, n_f),
        in_specs=in_specs,
        out_specs=out_specs,
        out_shape=out_shape,
        scratch_shapes=[pltpu.VMEM((tm, D_MODEL), F32)],
        compiler_params=pltpu.CompilerParams(
            dimension_semantics=("arbitrary", "arbitrary"), vmem_limit_bytes=VMEM_LIMIT),
        name="ffn",
    )(*args)


def kernel(x, attn_norm, w_in, sinks, out_norm_a, out_norm_b, w_out,
           ffn_norm, w_gate, w_up, w_down, final_norm):
    b, s_len, d = x.shape
    assert (s_len, d) == (SEQ, D_MODEL)
    depth = w_in.shape[0]

    idx = jnp.arange(N_ALIBI, dtype=F32)
    slopes = jnp.exp2(-8.0 * (idx + 1.0) / N_ALIBI)
    slopes_a, slopes_b = slopes[0::2], slopes[1::2]
    d16, d4, d1a, d1b = (jnp.asarray(p) for p in _dist_patterns())
    e1 = jnp.asarray(_ones_pattern(BLK), BF16)
    e2 = jnp.asarray(_ones_pattern(2 * BLK), BF16)

    wi, wo, wg, wu, wd = (w[0].astype(BF16) for w in (w_in, w_out, w_gate, w_up, w_down))

    xp = _permute(x, _to_residue_major_kernel, "to_residue_major").reshape(b * SEQ, D_MODEL)
    h = _norm(xp, attn_norm[0])
    for l in range(depth):
        last = l + 1 == depth
        *qkv_a, qb1, kvb1 = _in_proj(h, wi, b)
        out_a = _attn_a(qkv_a, slopes_a, d16, d4, d1a, e1, e2).reshape(b * SEQ, D_A)
        out_b = _attn_b(qb1, kvb1, slopes_b, sinks[l], d1b, e2).reshape(b * SEQ, D_B)
        xp, h, *nxt_proj = _out_proj(out_a, out_b, xp, out_norm_a[l], out_norm_b[l], wo,
                                     ffn_norm[l], None if last else (w_in, w_out, l + 1))
        xp, h, *nxt_ffn = _ffn(h, wg, wu, wd, xp, final_norm if last else attn_norm[l + 1],
                               F32 if last else BF16,
                               None if last else (w_gate, w_up, w_down, l + 1))
        if not last:
            (wi, wo), (wg, wu, wd) = nxt_proj, nxt_ffn
    return _permute(h.reshape(b, SEQ, D_MODEL), _from_residue_major_kernel, "from_residue_major")
```

```python
import math

import numpy as np
import jax
import jax.numpy as jnp
from jax import lax
from jax.experimental import pallas as pl
from jax.experimental.pallas import tpu as pltpu

F32 = jnp.float32
BF16 = jnp.bfloat16

D_MODEL = 2048
SEQ = 2048
HEAD_DIM = 64
N_HEADS_A = 16
N_HEADS_B = 16
N_KV_B = 2
D_A = N_HEADS_A * HEAD_DIM
D_B = N_HEADS_B * HEAD_DIM
D_KV_B = N_KV_B * HEAD_DIM
D_MIX = D_A + D_B
IN_COLS = 3 * D_A + D_B + 2 * D_KV_B
N_ALIBI = N_HEADS_A + N_HEADS_B
D_FF = 5632
EPS = 1e-6
SWA_WINDOW = 128
DIL_MAX_DIST = 128

LANES = 128
BLK = 128
N_RES = 16
N_J = SEQ // N_RES
N_BLK = SEQ // BLK
GROUP = 4
N_GROUPS = N_BLK // GROUP
LOG2E = math.log2(math.e)
VMEM_LIMIT = 56 * 1024 * 1024


def _rms(x, g):
    return x * lax.rsqrt(jnp.mean(x * x, axis=-1, keepdims=True) + EPS) * g


def _dist_patterns():
    inf = np.float32(np.inf)
    j = np.arange(BLK)
    d16 = (j[:, None] - j[None, :]).astype(np.float32)
    d16 = np.where(d16 >= 0, 16.0 * d16, inf).astype(np.float32)

    t = np.arange(BLK) // 32
    jj = np.arange(BLK) % 32
    sub = 4 * jj + t
    cur = sub[:, None] - sub[None, :]
    prev = cur + BLK
    d4 = np.concatenate([prev, cur], axis=1).astype(np.float32)
    d4 = np.where((d4 >= 0) & (d4 <= DIL_MAX_DIST), 4.0 * d4, inf).astype(np.float32)

    r = np.arange(BLK) // 8
    jj = np.arange(BLK) % 8
    pos = 16 * jj + r
    cur = pos[:, None] - pos[None, :]
    prev = cur + BLK
    d1 = np.concatenate([prev, cur], axis=1).astype(np.float32)
    d1a = np.where((d1 >= 0) & (d1 <= DIL_MAX_DIST), d1, inf).astype(np.float32)
    d1b = np.where((d1 >= 0) & (d1 <= SWA_WINDOW - 1), d1, inf).astype(np.float32)
    return d16, d4, d1a, d1b


def _ones_pattern(nk):
    e = np.zeros((2 * nk, LANES), np.float32)
    e[:nk, :HEAD_DIM] = 1.0
    e[nk:, HEAD_DIM:] = 1.0
    return e


def _to_residue_major_kernel(x_ref, o_ref):
    for r in range(N_RES):
        o_ref[r * N_J:(r + 1) * N_J, :] = x_ref[pl.ds(r, N_J, stride=N_RES), :]


def _from_residue_major_kernel(x_ref, o_ref):
    for r in range(N_RES):
        o_ref[pl.ds(r, N_J, stride=N_RES), :] = x_ref[r * N_J:(r + 1) * N_J, :]


def _permute(x, body, name):
    b = x.shape[0]
    slab = pl.BlockSpec((None, SEQ, LANES), lambda i, c: (i, 0, c))
    return pl.pallas_call(
        body,
        grid=(b, D_MODEL // LANES),
        in_specs=[slab],
        out_specs=slab,
        out_shape=jax.ShapeDtypeStruct(x.shape, F32),
        compiler_params=pltpu.CompilerParams(
            dimension_semantics=("arbitrary", "arbitrary"), vmem_limit_bytes=VMEM_LIMIT),
        name=name,
    )(x)


def _norm_kernel(x_ref, g_ref, h_ref):
    h_ref[...] = _rms(x_ref[...], g_ref[...]).astype(BF16)


def _norm(x, g):
    t = x.shape[0]
    tm = 512
    return pl.pallas_call(
        _norm_kernel,
        grid=(t // tm,),
        in_specs=[pl.BlockSpec((tm, D_MODEL), lambda m: (m, 0)),
                  pl.BlockSpec((1, D_MODEL), lambda m: (0, 0))],
        out_specs=pl.BlockSpec((tm, D_MODEL), lambda m: (m, 0)),
        out_shape=jax.ShapeDtypeStruct((t, D_MODEL), BF16),
        compiler_params=pltpu.CompilerParams(
            dimension_semantics=("arbitrary",), vmem_limit_bytes=VMEM_LIMIT),
        name="first_norm",
    )(x, g.reshape(1, D_MODEL))


IN_PROJ_RES = 2
IN_PROJ_TM = IN_PROJ_RES * N_J
IN_PROJ_STEPS = N_RES // IN_PROJ_RES
Q_SCALE = LOG2E * HEAD_DIM ** -0.5


def _in_proj_kernel(h_ref, w_ref, q16, k16l, k16h, v16l, v16h, q4, k4l, k4h, v4l, v4h,
                    q1, k1l, k1h, v1l, v1h, qb1, kvb1):
    h = h_ref[...]

    def order1(x):
        n = x.shape[1]
        pieces = [x[r * N_J + jb * 8:r * N_J + jb * 8 + 8]
                  for jb in range(N_J // 8) for r in range(IN_PROJ_RES)]
        return jnp.concatenate(pieces, axis=0).reshape(N_J // 8, 8 * IN_PROJ_RES, n)

    def proj(col, width, scale=None):
        x = jnp.dot(h, w_ref[:, col:col + width], preferred_element_type=F32)
        return x if scale is None else x * scale

    def store3(x, o16, o4, o1):
        o16[...] = x.astype(BF16)
        o4[...] = x.reshape(IN_PROJ_RES, 4, 32, D_A).astype(BF16)
        o1[...] = order1(x).astype(BF16)

    def first_head(width):
        return (lax.broadcasted_iota(jnp.int32, (IN_PROJ_TM, width), 1) & HEAD_DIM) == 0

    store3(proj(0, D_A, Q_SCALE), q16, q4, q1)
    lo = first_head(D_A)
    for i, outs in ((1, ((k16l, k4l, k1l), (k16h, k4h, k1h))),
                    (2, ((v16l, v4l, v1l), (v16h, v4h, v1h)))):
        x = proj(i * D_A, D_A)
        store3(jnp.where(lo, x, 0.0), *outs[0])
        store3(jnp.where(lo, 0.0, x), *outs[1])
    qb1[...] = order1(proj(3 * D_A, D_B, Q_SCALE)).astype(BF16)

    x = proj(3 * D_A + D_B, 2 * D_KV_B)
    lo = first_head(LANES)
    tiles = []
    for part in (x[:, :LANES], x[:, LANES:]):
        swapped = pltpu.roll(part, HEAD_DIM, 1)
        tiles += [jnp.where(lo, part, 0.0), jnp.where(lo, 0.0, swapped),
                  jnp.where(lo, swapped, 0.0), jnp.where(lo, 0.0, part)]
    kvb1[...] = order1(jnp.concatenate(tiles, axis=1)).astype(BF16)


def _in_proj(h, w, b):
    t = h.shape[0]
    per = IN_PROJ_STEPS
    o16 = pl.BlockSpec((IN_PROJ_TM, D_A), lambda m: (m, 0))
    o4 = pl.BlockSpec((None, IN_PROJ_RES, 4, None, 32, D_A),
                      lambda m: (m // per, (m % per) % 2, 0, (m % per) // 2, 0, 0))
    o1 = lambda c: pl.BlockSpec((None, N_J // 8, None, 8 * IN_PROJ_RES, c),
                                lambda m: (m // per, 0, m % per, 0, 0))
    s16 = jax.ShapeDtypeStruct((t, D_A), BF16)
    s4 = jax.ShapeDtypeStruct((b, 4, 4, 4, 32, D_A), BF16)
    s1 = lambda c: jax.ShapeDtypeStruct((b, N_J // 8, per, 8 * IN_PROJ_RES, c), BF16)
    outs = pl.pallas_call(
        _in_proj_kernel,
        grid=(t // IN_PROJ_TM,),
        in_specs=[pl.BlockSpec((IN_PROJ_TM, D_MODEL), lambda m: (m, 0)),
                  pl.BlockSpec((D_MODEL, IN_COLS), lambda m: (0, 0),
                               pipeline_mode=pl.Buffered(1))],
        out_specs=[o16] * 5 + [o4] * 5 + [o1(D_A)] * 5 + [o1(D_B), o1(8 * LANES)],
        out_shape=[s16] * 5 + [s4] * 5 + [s1(D_A)] * 5 + [s1(D_B), s1(8 * LANES)],
        compiler_params=pltpu.CompilerParams(
            dimension_semantics=("arbitrary",), vmem_limit_bytes=VMEM_LIMIT),
        name="in_proj",
    )(h, w)
    return [o.reshape(b, SEQ, o.shape[-1]) for o in outs]


def _lane_lo():
    return lax.broadcasted_iota(jnp.int32, (BLK, LANES), 1) < HEAD_DIM


def _kv_window(pair, start, size):
    return jnp.concatenate([ref[pl.ds(start, size), :] for ref in pair], axis=0)


def _scores_softmax(qs, ks, q_starts, k_starts, nk, bias0, bias1):
    q = jnp.stack([qs[pl.ds(s, BLK), :] for s in q_starts])
    kk = jnp.stack([_kv_window(ks, s, nk) for s in k_starts])
    s = jnp.einsum("gqd,gkd->gqk", q, kk, preferred_element_type=F32)
    l0 = s[:, :, :nk] + bias0
    l1 = s[:, :, nk:] + bias1
    m0 = jnp.max(l0, axis=2, keepdims=True)
    m1 = jnp.max(l1, axis=2, keepdims=True)
    p = jnp.concatenate([jnp.exp2(l0 - m0), jnp.exp2(l1 - m1)], axis=2).astype(BF16)
    return p, m0, m1


def _weighted_values(p, vs, ones_ref, k_starts, nk):
    ones = ones_ref[...]
    vv = jnp.stack([jnp.concatenate([_kv_window(vs, s, nk), ones], axis=1) for s in k_starts])
    na = jnp.einsum("gqk,gkd->gqd", p, vv, preferred_element_type=F32)
    return na[:, :, :LANES], na[:, :, LANES:]


def _is_first(row, period):
    return period is not None and row % period == 0


def _band_starts(rows, period):
    if period is None:
        return rows
    return [r if _is_first(r, period) else r - BLK for r in rows]


def _band_bias(rows, period, bias, bias_first):
    if not any(_is_first(r, period) for r in rows):
        return bias[0], bias[1]
    return tuple(jnp.stack([(bias_first if _is_first(r, period) else bias)[h] for r in rows])
                 for h in range(2))


def _pipelined(n_groups, first_half, second_half):
    first_half(0, 0)
    for g in range(n_groups - 1):
        first_half(g + 1, (g + 1) % 2)
        second_half(g, g % 2)
    second_half(n_groups - 1, (n_groups - 1) % 2)


def _first_block_bias(bias, bias_first):
    for h in range(2):
        bias_first[h, :, :BLK] = bias[h, :, BLK:]
        bias_first[h, :, BLK:] = jnp.full((BLK, BLK), -jnp.inf, F32)


def _attn_a_kernel(slopes_ref, q16, k16l, k16h, v16l, v16h, q4, k4l, k4h, v4l, v4h,
                   q1, k1l, k1h, v1l, v1h, d16_ref, d4_ref, d1_ref, e1_ref, e2_ref,
                   o_ref, b16, b4, b4f, b1, b1f, num_s, m_s, den_s, p_scr):
    hp = pl.program_id(1)
    lo = _lane_lo()
    ns0 = -LOG2E * slopes_ref[2 * hp]
    ns1 = -LOG2E * slopes_ref[2 * hp + 1]
    k16, v16, k4, v4, k1, v1 = ((k16l, k16h), (v16l, v16h), (k4l, k4h), (v4l, v4h),
                                (k1l, k1h), (v1l, v1h))

    for bias, dist in ((b16, d16_ref), (b4, d4_ref), (b1, d1_ref)):
        bias[0] = ns0 * dist[...]
        bias[1] = ns1 * dist[...]
    _first_block_bias(b4, b4f)
    _first_block_bias(b1, b1f)

    def scatter(dst_scr, branch, x, dsts, size):
        for c, d in enumerate(dsts):
            dst_scr[branch, pl.ds(d, size), :] = x[c * size:(c + 1) * size]

    def blocks_of(g):
        return [(g * GROUP + i) * BLK for i in range(GROUP)]

    def branch_halves(qs, ks, vs, ones, nk, bias, bias_first, period, dsts, size, keep, emit):
        def first_half(g, slot):
            rows = blocks_of(g)
            bias0, bias1 = _band_bias(rows, period, bias, bias_first)
            p, m0, m1 = _scores_softmax(qs, ks, rows, _band_starts(rows, period), nk,
                                        bias0, bias1)
            p_scr[slot, :, :, :2 * nk] = p
            for i in range(GROUP):
                keep(g, slot, i, jnp.where(lo, m0[i], m1[i]))

        def second_half(g, slot):
            rows = blocks_of(g)
            num, den = _weighted_values(p_scr[slot, :, :, :2 * nk], vs, ones,
                                        _band_starts(rows, period), nk)
            for i in range(GROUP):
                emit(g, slot, i, num[i], den[i])

        return first_half, second_half

    def to_scratch(branch, dsts, size):
        def keep(g, slot, i, m):
            scatter(m_s, branch, m, dsts(g, i), size)

        def emit(g, slot, i, num, den):
            scatter(num_s, branch, num, dsts(g, i), size)
            scatter(den_s, branch, den, dsts(g, i), size)

        return keep, emit

    def dst16(g, i):
        return [(g * GROUP + i) * BLK]

    def dst4(g, i):
        r4, mb = divmod(g * GROUP + i, 4)
        return [(r4 + 4 * t) * N_J + 32 * mb for t in range(4)]

    def dst1(g, i):
        blk = g * GROUP + i
        return [r * N_J + 8 * blk for r in range(N_RES)]

    _pipelined(N_GROUPS, *branch_halves(q16, k16, v16, e1_ref, BLK, b16, None, None,
                                        dst16, BLK, *to_scratch(2, dst16, BLK)))
    _pipelined(N_GROUPS, *branch_halves(q4, k4, v4, e2_ref, 2 * BLK, b4, b4f, SEQ // 4,
                                        dst4, 32, *to_scratch(1, dst4, 32)))
    _pipelined(N_GROUPS, *branch_halves(q1, k1, v1, e2_ref, 2 * BLK, b1, b1f, SEQ,
                                        dst1, 8, *to_scratch(0, dst1, 8)))

    def merge(g, carry):
        rows = pl.ds(pl.multiple_of(g * BLK, BLK), BLK)
        m = [m_s[i, rows, :] for i in range(3)]
        m_all = jnp.maximum(jnp.maximum(m[0], m[1]), m[2])
        w = [jnp.exp2(mi - m_all) for mi in m]
        numer = num_s[0, rows, :] * w[0] + num_s[1, rows, :] * w[1] + num_s[2, rows, :] * w[2]
        denom = den_s[0, rows, :] * w[0] + den_s[1, rows, :] * w[1] + den_s[2, rows, :] * w[2]
        o_ref[rows, :] = numer / denom
        return carry

    lax.fori_loop(0, N_BLK, merge, 0, unroll=2)


def _attn_a(qkv, slopes_a, d16, d4, d1a, e1, e2):
    b = qkv[0].shape[0]
    n_pairs = N_HEADS_A // 2
    col = pl.BlockSpec((None, SEQ, LANES), lambda i, hp: (i, 0, hp))
    const = lambda shape: pl.BlockSpec(shape, lambda i, hp: (0, 0))
    band = pltpu.VMEM((2, BLK, 2 * BLK), F32)
    stat = pltpu.VMEM((3, SEQ, LANES), F32)
    return pl.pallas_call(
        _attn_a_kernel,
        grid=(b, n_pairs),
        in_specs=[pl.BlockSpec(memory_space=pltpu.SMEM)] + [col] * 15 + [
            const((BLK, BLK)), const((BLK, 2 * BLK)), const((BLK, 2 * BLK)),
            const((2 * BLK, LANES)), const((4 * BLK, LANES))],
        out_specs=pl.BlockSpec((None, SEQ, LANES), lambda i, hp: (i, 0, hp)),
        out_shape=jax.ShapeDtypeStruct((b, SEQ, D_A), F32),
        scratch_shapes=[pltpu.VMEM((2, BLK, BLK), F32), band, band, band, band,
                        stat, stat, stat, pltpu.VMEM((2, GROUP, BLK, 4 * BLK), BF16)],
        compiler_params=pltpu.CompilerParams(
            dimension_semantics=("arbitrary", "arbitrary"), vmem_limit_bytes=VMEM_LIMIT),
        name="attn_dilated",
    )(slopes_a, *qkv, d16, d4, d1a, e1, e2)


def _attn_b_kernel(slopes_ref, sinks_ref, q1, k1l, k1h, v1l, v1h, d1_ref, e2_ref, o_ref,
                   b1, b1f, p_scr, m_scr):
    hp = pl.program_id(1)
    lo = _lane_lo()
    sink0 = LOG2E * sinks_ref[2 * hp]
    sink1 = LOG2E * sinks_ref[2 * hp + 1]
    b1[0] = (-LOG2E * slopes_ref[2 * hp]) * d1_ref[...]
    b1[1] = (-LOG2E * slopes_ref[2 * hp + 1]) * d1_ref[...]
    _first_block_bias(b1, b1f)
    k1, v1 = (k1l, k1h), (v1l, v1h)

    sink = jnp.where(lo, sink0, sink1)

    def blocks_of(g):
        return [(g * GROUP + i) * BLK for i in range(GROUP)]

    def first_half(g, slot):
        rows = blocks_of(g)
        bias0, bias1 = _band_bias(rows, SEQ, b1, b1f)
        p, m0, m1 = _scores_softmax(q1, k1, rows, _band_starts(rows, SEQ), 2 * BLK,
                                    bias0, bias1)
        p_scr[slot] = p
        for i in range(GROUP):
            m_scr[slot, i] = jnp.where(lo, m0[i], m1[i])

    def second_half(g, slot):
        rows = blocks_of(g)
        num, den = _weighted_values(p_scr[slot], v1, e2_ref,
                                    _band_starts(rows, SEQ), 2 * BLK)
        for i in range(GROUP):
            m = m_scr[slot, i]
            m2 = jnp.maximum(m, sink)
            c = jnp.exp2(m - m2)
            out = (num[i] * c) / (den[i] * c + jnp.exp2(sink - m2))
            blk = g * GROUP + i
            for r in range(N_RES):
                o_ref[pl.ds(r * N_J + 8 * blk, 8), :] = out[r * 8:(r + 1) * 8]

    _pipelined(N_GROUPS, first_half, second_half)


def _attn_b(q1, kv1, slopes_b, sinks, d1b, e2):
    b = q1.shape[0]
    n_pairs = N_HEADS_B // 2
    pairs_per_kv = n_pairs // N_KV_B
    kv_tile = lambda tile: pl.BlockSpec(
        (None, SEQ, LANES), lambda i, hp, tile=tile: (i, 0, tile + 2 * (hp // pairs_per_kv)))
    band = pltpu.VMEM((2, BLK, 2 * BLK), F32)
    return pl.pallas_call(
        _attn_b_kernel,
        grid=(b, n_pairs),
        in_specs=[pl.BlockSpec(memory_space=pltpu.SMEM),
                  pl.BlockSpec(memory_space=pltpu.SMEM),
                  pl.BlockSpec((None, SEQ, LANES), lambda i, hp: (i, 0, hp)),
                  kv_tile(0), kv_tile(1), kv_tile(4), kv_tile(5),
                  pl.BlockSpec((BLK, 2 * BLK), lambda i, hp: (0, 0)),
                  pl.BlockSpec((4 * BLK, LANES), lambda i, hp: (0, 0))],
        out_specs=pl.BlockSpec((None, SEQ, LANES), lambda i, hp: (i, 0, hp)),
        out_shape=jax.ShapeDtypeStruct((b, SEQ, D_B), F32),
        scratch_shapes=[band, band, pltpu.VMEM((2, GROUP, BLK, 4 * BLK), BF16),
                        pltpu.VMEM((2, GROUP, BLK, LANES), F32)],
        compiler_params=pltpu.CompilerParams(
            dimension_semantics=("arbitrary", "arbitrary"), vmem_limit_bytes=VMEM_LIMIT),
        name="attn_swa",
    )(slopes_b, sinks, q1, kv1, kv1, kv1, kv1, d1b, e2)


def _cast_chunks(src_refs, dst_refs):
    for src, dst in zip(src_refs, dst_refs):
        dst[...] = src[...].astype(BF16)


def _out_proj_kernel(a_ref, b_ref, x_ref, ga_ref, gb_ref, w_ref, gf_ref, *rest):
    n_cast = (len(rest) - 2) // 2
    cast_src, (xo_ref, h_ref), cast_dst = rest[:n_cast], rest[n_cast:n_cast + 2], rest[n_cast + 2:]
    na = _rms(a_ref[...], ga_ref[...]).astype(BF16)
    nb = _rms(b_ref[...], gb_ref[...]).astype(BF16)
    mix = jnp.concatenate([na, nb], axis=1)
    x = x_ref[...] + jnp.dot(mix, w_ref[...], preferred_element_type=F32)
    xo_ref[...] = x
    h_ref[...] = _rms(x, gf_ref[...]).astype(BF16)
    _cast_chunks(cast_src, cast_dst)


def _out_proj(a, b, x, ga, gb, w, gf, nxt):
    t = x.shape[0]
    tm = 512
    steps = t // tm
    row = lambda n: pl.BlockSpec((tm, n), lambda m: (m, 0))
    const = lambda r, c: pl.BlockSpec((r, c), lambda m: (0, 0))
    in_specs = [row(D_A), row(D_B), row(D_MODEL), const(1, D_A), const(1, D_B),
                pl.BlockSpec((D_MIX, D_MODEL), lambda m: (0, 0), pipeline_mode=pl.Buffered(1)),
                const(1, D_MODEL)]
    out_specs = [row(D_MODEL), row(D_MODEL)]
    out_shape = [jax.ShapeDtypeStruct((t, D_MODEL), F32), jax.ShapeDtypeStruct((t, D_MODEL), BF16)]
    args = [a, b, x, ga.reshape(1, D_A), gb.reshape(1, D_B), w, gf.reshape(1, D_MODEL)]
    if nxt is not None:
        *stacked, layer = nxt
        for wf in stacked:
            rows, cols = wf.shape[1] // steps, wf.shape[2]
            in_specs.append(pl.BlockSpec((None, rows, cols), lambda m: (layer, m, 0)))
            out_specs.append(pl.BlockSpec((rows, cols), lambda m: (m, 0)))
            out_shape.append(jax.ShapeDtypeStruct(wf.shape[1:], BF16))
        args += stacked
    return pl.pallas_call(
        _out_proj_kernel,
        grid=(steps,),
        in_specs=in_specs,
        out_specs=out_specs,
        out_shape=out_shape,
        compiler_params=pltpu.CompilerParams(
            dimension_semantics=("arbitrary",), vmem_limit_bytes=VMEM_LIMIT),
        name="out_proj",
    )(*args)


GATE_UP_TM, GATE_UP_TF = 2048, 512
DOWN_TM = 256


def _gate_up_kernel(h_ref, wg_ref, wu_ref, *rest):
    n_cast = (len(rest) - 1) // 2
    cast_src, act_ref, cast_dst = rest[:n_cast], rest[n_cast], rest[n_cast + 1:]
    h = h_ref[...]
    gate = jnp.dot(h, wg_ref[...], preferred_element_type=F32)
    up = jnp.dot(h, wu_ref[...], preferred_element_type=F32)
    act_ref[...] = (jax.nn.silu(gate) * up).astype(BF16)
    _cast_chunks(cast_src, cast_dst)


def _gate_up(h, wg, wu, nxt):
    t = h.shape[0]
    tm, tf = GATE_UP_TM, GATE_UP_TF
    n_m, n_f = t // tm, D_FF // tf
    w_spec = pl.BlockSpec((D_MODEL, tf), lambda m, f: (0, f))
    in_specs = [pl.BlockSpec((tm, D_MODEL), lambda m, f: (m, 0)), w_spec, w_spec]
    out_specs = [pl.BlockSpec((tm, tf), lambda m, f: (m, f))]
    out_shape = [jax.ShapeDtypeStruct((t, D_FF), BF16)]
    args = [h, wg, wu]
    if nxt is not None:
        *stacked, layer = nxt
        chunk = (D_MODEL // n_m, D_FF // n_f)
        for wf in stacked:
            in_specs.append(pl.BlockSpec((None,) + chunk, lambda m, f: (layer, m, f)))
            out_specs.append(pl.BlockSpec(chunk, lambda m, f: (m, f)))
            out_shape.append(jax.ShapeDtypeStruct(wf.shape[1:], BF16))
        args += stacked
    return pl.pallas_call(
        _gate_up_kernel,
        grid=(n_m, n_f),
        in_specs=in_specs,
        out_specs=out_specs,
        out_shape=out_shape,
        compiler_params=pltpu.CompilerParams(
            dimension_semantics=("arbitrary", "arbitrary"), vmem_limit_bytes=VMEM_LIMIT),
        name="ffn_gate_up",
    )(*args)


def _down_kernel(act_ref, wd_ref, x_ref, gn_ref, *rest):
    n_cast = (len(rest) - 2) // 2
    cast_src, (xo_ref, hn_ref), cast_dst = rest[:n_cast], rest[n_cast:n_cast + 2], rest[n_cast + 2:]
    x = x_ref[...] + jnp.dot(act_ref[...], wd_ref[...], preferred_element_type=F32)
    xo_ref[...] = x
    hn_ref[...] = _rms(x, gn_ref[...]).astype(hn_ref.dtype)
    _cast_chunks(cast_src, cast_dst)


def _down(act, wd, x, gn, normed_dtype, nxt):
    t = x.shape[0]
    tm = DOWN_TM
    steps = t // tm
    row = lambda n: pl.BlockSpec((tm, n), lambda m: (m, 0))
    in_specs = [row(D_FF),
                pl.BlockSpec((D_FF, D_MODEL), lambda m: (0, 0), pipeline_mode=pl.Buffered(1)),
                row(D_MODEL), pl.BlockSpec((1, D_MODEL), lambda m: (0, 0))]
    out_specs = [row(D_MODEL), row(D_MODEL)]
    out_shape = [jax.ShapeDtypeStruct((t, D_MODEL), F32),
                 jax.ShapeDtypeStruct((t, D_MODEL), normed_dtype)]
    args = [act, wd, x, gn.reshape(1, D_MODEL)]
    if nxt is not None:
        w_down, layer = nxt
        chunk = (D_FF // steps, D_MODEL)
        in_specs.append(pl.BlockSpec((None,) + chunk, lambda m: (layer, m, 0)))
        out_specs.append(pl.BlockSpec(chunk, lambda m: (m, 0)))
        out_shape.append(jax.ShapeDtypeStruct(w_down.shape[1:], BF16))
        args.append(w_down)
    return pl.pallas_call(
        _down_kernel,
        grid=(steps,),
        in_specs=in_specs,
        out_specs=out_specs,
        out_shape=out_shape,
        compiler_params=pltpu.CompilerParams(
            dimension_semantics=("arbitrary",), vmem_limit_bytes=VMEM_LIMIT),
        name="ffn_down",
    )(*args)


def kernel(x, attn_norm, w_in, sinks, out_norm_a, out_norm_b, w_out,
           ffn_norm, w_gate, w_up, w_down, final_norm):
    b, s_len, d = x.shape
    assert (s_len, d) == (SEQ, D_MODEL)
    depth = w_in.shape[0]

    idx = jnp.arange(N_ALIBI, dtype=F32)
    slopes = jnp.exp2(-8.0 * (idx + 1.0) / N_ALIBI)
    slopes_a, slopes_b = slopes[0::2], slopes[1::2]
    d16, d4, d1a, d1b = (jnp.asarray(p) for p in _dist_patterns())
    e1 = jnp.asarray(_ones_pattern(BLK), BF16)
    e2 = jnp.asarray(_ones_pattern(2 * BLK), BF16)

    wi, wo, wg, wu, wd = (w[0].astype(BF16) for w in (w_in, w_out, w_gate, w_up, w_down))

    xp = _permute(x, _to_residue_major_kernel, "to_residue_major").reshape(b * SEQ, D_MODEL)
    h = _norm(xp, attn_norm[0])
    for l in range(depth):
        last = l + 1 == depth
        *qkv_a, qb1, kvb1 = _in_proj(h, wi, b)
        out_a = _attn_a(qkv_a, slopes_a, d16, d4, d1a, e1, e2).reshape(b * SEQ, D_A)
        out_b = _attn_b(qb1, kvb1, slopes_b, sinks[l], d1b, e2).reshape(b * SEQ, D_B)
        xp, h, *nxt_proj = _out_proj(out_a, out_b, xp, out_norm_a[l], out_norm_b[l], wo,
                                     ffn_norm[l], None if last else (w_in, w_out, l + 1))
        act, *nxt_gu = _gate_up(h, wg, wu, None if last else (w_gate, w_up, l + 1))
        xp, h, *nxt_d = _down(act, wd, xp, final_norm if last else attn_norm[l + 1],
                              F32 if last else BF16, None if last else (w_down, l + 1))
        if not last:
            (wi, wo), (wg, wu), (wd,) = nxt_proj, nxt_gu, nxt_d
    return _permute(h.reshape(b, SEQ, D_MODEL), _from_residue_major_kernel, "from_residue_major")
```

```python
import math

import numpy as np
import jax
import jax.numpy as jnp
from jax import lax
from jax.experimental import pallas as pl
from jax.experimental.pallas import tpu as pltpu

F32 = jnp.float32
BF16 = jnp.bfloat16

D_MODEL = 2048
SEQ = 2048
HEAD_DIM = 64
N_HEADS_A = 16
N_HEADS_B = 16
N_KV_B = 2
D_A = N_HEADS_A * HEAD_DIM
D_B = N_HEADS_B * HEAD_DIM
D_KV_B = N_KV_B * HEAD_DIM
D_MIX = D_A + D_B
IN_COLS = 3 * D_A + D_B + 2 * D_KV_B
N_ALIBI = N_HEADS_A + N_HEADS_B
D_FF = 5632
EPS = 1e-6
SWA_WINDOW = 128
DIL_MAX_DIST = 128

LANES = 128
BLK = 128
N_RES = 16
N_J = SEQ // N_RES
N_BLK = SEQ // BLK
GROUP = 4
N_GROUPS = N_BLK // GROUP
LOG2E = math.log2(math.e)
VMEM_LIMIT = 56 * 1024 * 1024


def _rms(x, g):
    return x * lax.rsqrt(jnp.mean(x * x, axis=-1, keepdims=True) + EPS) * g


def _dist_patterns():
    inf = np.float32(np.inf)
    j = np.arange(BLK)
    d16 = (j[:, None] - j[None, :]).astype(np.float32)
    d16 = np.where(d16 >= 0, 16.0 * d16, inf).astype(np.float32)

    t = np.arange(BLK) // 32
    jj = np.arange(BLK) % 32
    sub = 4 * jj + t
    cur = sub[:, None] - sub[None, :]
    prev = cur + BLK
    d4 = np.concatenate([prev, cur], axis=1).astype(np.float32)
    d4 = np.where((d4 >= 0) & (d4 <= DIL_MAX_DIST), 4.0 * d4, inf).astype(np.float32)

    r = np.arange(BLK) // 8
    jj = np.arange(BLK) % 8
    pos = 16 * jj + r
    cur = pos[:, None] - pos[None, :]
    prev = cur + BLK
    d1 = np.concatenate([prev, cur], axis=1).astype(np.float32)
    d1a = np.where((d1 >= 0) & (d1 <= DIL_MAX_DIST), d1, inf).astype(np.float32)
    d1b = np.where((d1 >= 0) & (d1 <= SWA_WINDOW - 1), d1, inf).astype(np.float32)
    return d16, d4, d1a, d1b


def _ones_pattern(nk):
    e = np.zeros((2 * nk, LANES), np.float32)
    e[:nk, :HEAD_DIM] = 1.0
    e[nk:, HEAD_DIM:] = 1.0
    return e


def _to_residue_major_kernel(x_ref, o_ref):
    for r in range(N_RES):
        o_ref[r * N_J:(r + 1) * N_J, :] = x_ref[pl.ds(r, N_J, stride=N_RES), :]


def _from_residue_major_kernel(x_ref, o_ref):
    for r in range(N_RES):
        o_ref[pl.ds(r, N_J, stride=N_RES), :] = x_ref[r * N_J:(r + 1) * N_J, :]


def _permute(x, body, name):
    b = x.shape[0]
    slab = pl.BlockSpec((None, SEQ, LANES), lambda i, c: (i, 0, c))
    return pl.pallas_call(
        body,
        grid=(b, D_MODEL // LANES),
        in_specs=[slab],
        out_specs=slab,
        out_shape=jax.ShapeDtypeStruct(x.shape, F32),
        compiler_params=pltpu.CompilerParams(
            dimension_semantics=("arbitrary", "arbitrary"), vmem_limit_bytes=VMEM_LIMIT),
        name=name,
    )(x)


def _norm_kernel(x_ref, g_ref, h_ref):
    h_ref[...] = _rms(x_ref[...], g_ref[...]).astype(BF16)


def _norm(x, g):
    t = x.shape[0]
    tm = 512
    return pl.pallas_call(
        _norm_kernel,
        grid=(t // tm,),
        in_specs=[pl.BlockSpec((tm, D_MODEL), lambda m: (m, 0)),
                  pl.BlockSpec((1, D_MODEL), lambda m: (0, 0))],
        out_specs=pl.BlockSpec((tm, D_MODEL), lambda m: (m, 0)),
        out_shape=jax.ShapeDtypeStruct((t, D_MODEL), BF16),
        compiler_params=pltpu.CompilerParams(
            dimension_semantics=("arbitrary",), vmem_limit_bytes=VMEM_LIMIT),
        name="first_norm",
    )(x, g.reshape(1, D_MODEL))


IN_PROJ_RES = 2
IN_PROJ_TM = IN_PROJ_RES * N_J
IN_PROJ_STEPS = N_RES // IN_PROJ_RES
Q_SCALE = LOG2E * HEAD_DIM ** -0.5


def _in_proj_kernel(h_ref, w_ref, q16, k16l, k16h, v16l, v16h, q1, k1l, k1h, v1l, v1h,
                    qb1, kvb1):
    h = h_ref[...]

    def order1(x):
        n = x.shape[1]
        pieces = [x[r * N_J + jb * 8:r * N_J + jb * 8 + 8]
                  for jb in range(N_J // 8) for r in range(IN_PROJ_RES)]
        return jnp.concatenate(pieces, axis=0).reshape(N_J // 8, 8 * IN_PROJ_RES, n)

    def proj(col, width, scale=None):
        x = jnp.dot(h, w_ref[:, col:col + width], preferred_element_type=F32)
        return x if scale is None else x * scale

    def store2(x, o16, o1):
        o16[...] = x.astype(BF16)
        o1[...] = order1(x).astype(BF16)

    def first_head(width):
        return (lax.broadcasted_iota(jnp.int32, (IN_PROJ_TM, width), 1) & HEAD_DIM) == 0

    store2(proj(0, D_A, Q_SCALE), q16, q1)
    lo = first_head(D_A)
    for i, outs in ((1, ((k16l, k1l), (k16h, k1h))), (2, ((v16l, v1l), (v16h, v1h)))):
        x = proj(i * D_A, D_A)
        store2(jnp.where(lo, x, 0.0), *outs[0])
        store2(jnp.where(lo, 0.0, x), *outs[1])
    qb1[...] = order1(proj(3 * D_A, D_B, Q_SCALE)).astype(BF16)

    x = proj(3 * D_A + D_B, 2 * D_KV_B)
    lo = first_head(LANES)
    tiles = []
    for part in (x[:, :LANES], x[:, LANES:]):
        swapped = pltpu.roll(part, HEAD_DIM, 1)
        tiles += [jnp.where(lo, part, 0.0), jnp.where(lo, 0.0, swapped),
                  jnp.where(lo, swapped, 0.0), jnp.where(lo, 0.0, part)]
    kvb1[...] = order1(jnp.concatenate(tiles, axis=1)).astype(BF16)


def _in_proj(h, w, b):
    t = h.shape[0]
    per = IN_PROJ_STEPS
    o16 = pl.BlockSpec((IN_PROJ_TM, D_A), lambda m: (m, 0))
    o1 = lambda c: pl.BlockSpec((None, N_J // 8, None, 8 * IN_PROJ_RES, c),
                                lambda m: (m // per, 0, m % per, 0, 0))
    s16 = jax.ShapeDtypeStruct((t, D_A), BF16)
    s1 = lambda c: jax.ShapeDtypeStruct((b, N_J // 8, per, 8 * IN_PROJ_RES, c), BF16)
    outs = pl.pallas_call(
        _in_proj_kernel,
        grid=(t // IN_PROJ_TM,),
        in_specs=[pl.BlockSpec((IN_PROJ_TM, D_MODEL), lambda m: (m, 0)),
                  pl.BlockSpec((D_MODEL, IN_COLS), lambda m: (0, 0),
                               pipeline_mode=pl.Buffered(1))],
        out_specs=[o16] * 5 + [o1(D_A)] * 5 + [o1(D_B), o1(8 * LANES)],
        out_shape=[s16] * 5 + [s1(D_A)] * 5 + [s1(D_B), s1(8 * LANES)],
        compiler_params=pltpu.CompilerParams(
            dimension_semantics=("arbitrary",), vmem_limit_bytes=VMEM_LIMIT),
        name="in_proj",
    )(h, w)
    return [o.reshape(b, SEQ, o.shape[-1]) for o in outs]


def _lane_lo():
    return lax.broadcasted_iota(jnp.int32, (BLK, LANES), 1) < HEAD_DIM


def _rows(ref, chunks):
    return jnp.concatenate([ref[pl.ds(start, size), :] for start, size in chunks], axis=0)


def _window(pair, window, src):
    chunks = [c for blk in window for c in src(blk)]
    return jnp.concatenate([_rows(ref, chunks) for ref in pair], axis=0)


def _scores_softmax(qs, ks, src, blocks, windows, bias0, bias1):
    nk = BLK * len(windows[0])
    q = jnp.stack([_rows(qs, src(blk)) for blk in blocks])
    kk = jnp.stack([_window(ks, w, src) for w in windows])
    s = jnp.einsum("gqd,gkd->gqk", q, kk, preferred_element_type=F32)
    l0 = s[:, :, :nk] + bias0
    l1 = s[:, :, nk:] + bias1
    m0 = jnp.max(l0, axis=2, keepdims=True)
    m1 = jnp.max(l1, axis=2, keepdims=True)
    p = jnp.concatenate([jnp.exp2(l0 - m0), jnp.exp2(l1 - m1)], axis=2).astype(BF16)
    return p, m0, m1


def _weighted_values(p, vs, ones_ref, src, windows):
    ones = ones_ref[...]
    vv = jnp.stack([jnp.concatenate([_window(vs, w, src), ones], axis=1) for w in windows])
    na = jnp.einsum("gqk,gkd->gqd", p, vv, preferred_element_type=F32)
    return na[:, :, :LANES], na[:, :, LANES:]


def _is_first(blk, per):
    return per is not None and blk % per == 0


def _band_windows(blocks, per):
    if per is None:
        return [(blk,) for blk in blocks]
    return [(blk, blk + 1) if _is_first(blk, per) else (blk - 1, blk) for blk in blocks]


def _band_bias(blocks, per, bias, bias_first):
    if not any(_is_first(blk, per) for blk in blocks):
        return bias[0], bias[1]
    return tuple(jnp.stack([(bias_first if _is_first(blk, per) else bias)[h] for blk in blocks])
                 for h in range(2))


def _pipelined(n_groups, first_half, second_half):
    first_half(0, 0)
    for g in range(n_groups - 1):
        first_half(g + 1, (g + 1) % 2)
        second_half(g, g % 2)
    second_half(n_groups - 1, (n_groups - 1) % 2)


def _first_block_bias(bias, bias_first):
    for h in range(2):
        bias_first[h, :, :BLK] = bias[h, :, BLK:]
        bias_first[h, :, BLK:] = jnp.full((BLK, BLK), -jnp.inf, F32)


def _attn_a_kernel(slopes_ref, q16, k16l, k16h, v16l, v16h, q1, k1l, k1h, v1l, v1h,
                   d16_ref, d4_ref, d1_ref, e1_ref, e2_ref,
                   o_ref, b16, b4, b4f, b1, b1f, num_s, m_s, den_s, p_scr):
    hp = pl.program_id(1)
    lo = _lane_lo()
    ns0 = -LOG2E * slopes_ref[2 * hp]
    ns1 = -LOG2E * slopes_ref[2 * hp + 1]
    k16, v16, k1, v1 = (k16l, k16h), (v16l, v16h), (k1l, k1h), (v1l, v1h)

    for bias, dist in ((b16, d16_ref), (b4, d4_ref), (b1, d1_ref)):
        bias[0] = ns0 * dist[...]
        bias[1] = ns1 * dist[...]
    _first_block_bias(b4, b4f)
    _first_block_bias(b1, b1f)

    def chunks16(blk):
        return [(blk * BLK, BLK)]

    def chunks4(blk):
        r4, mb = divmod(blk, 4)
        return [((r4 + 4 * t) * N_J + 32 * mb, 32) for t in range(4)]

    def chunks1_residue_major(blk):
        return [(r * N_J + 8 * blk, 8) for r in range(N_RES)]

    def scatter(dst_scr, branch, x, chunks):
        row = 0
        for start, size in chunks:
            dst_scr[branch, pl.ds(start, size), :] = x[row:row + size]
            row += size

    def blocks_of(g):
        return [g * GROUP + i for i in range(GROUP)]

    def branch_halves(qs, ks, vs, ones, src, dst, bias, bias_first, per, branch):
        def first_half(g, slot):
            blocks = blocks_of(g)
            windows = _band_windows(blocks, per)
            bias0, bias1 = _band_bias(blocks, per, bias, bias_first)
            p, m0, m1 = _scores_softmax(qs, ks, src, blocks, windows, bias0, bias1)
            p_scr[slot, :, :, :p.shape[2]] = p
            for i, blk in enumerate(blocks):
                scatter(m_s, branch, jnp.where(lo, m0[i], m1[i]), dst(blk))

        def second_half(g, slot):
            blocks = blocks_of(g)
            windows = _band_windows(blocks, per)
            width = 2 * BLK * len(windows[0])
            num, den = _weighted_values(p_scr[slot, :, :, :width], vs, ones, src, windows)
            for i, blk in enumerate(blocks):
                scatter(num_s, branch, num[i], dst(blk))
                scatter(den_s, branch, den[i], dst(blk))

        return first_half, second_half

    _pipelined(N_GROUPS, *branch_halves(q16, k16, v16, e1_ref, chunks16, chunks16,
                                        b16, None, None, 2))
    _pipelined(N_GROUPS, *branch_halves(q16, k16, v16, e2_ref, chunks4, chunks4,
                                        b4, b4f, 4, 1))
    _pipelined(N_GROUPS, *branch_halves(q1, k1, v1, e2_ref, chunks16, chunks1_residue_major,
                                        b1, b1f, N_BLK, 0))

    def merge(g, carry):
        rows = pl.ds(pl.multiple_of(g * BLK, BLK), BLK)
        m = [m_s[i, rows, :] for i in range(3)]
        m_all = jnp.maximum(jnp.maximum(m[0], m[1]), m[2])
        w = [jnp.exp2(mi - m_all) for mi in m]
        numer = num_s[0, rows, :] * w[0] + num_s[1, rows, :] * w[1] + num_s[2, rows, :] * w[2]
        denom = den_s[0, rows, :] * w[0] + den_s[1, rows, :] * w[1] + den_s[2, rows, :] * w[2]
        o_ref[rows, :] = numer / denom
        return carry

    lax.fori_loop(0, N_BLK, merge, 0, unroll=2)


def _attn_a(qkv, slopes_a, d16, d4, d1a, e1, e2):
    b = qkv[0].shape[0]
    n_pairs = N_HEADS_A // 2
    col = pl.BlockSpec((None, SEQ, LANES), lambda i, hp: (i, 0, hp))
    const = lambda shape: pl.BlockSpec(shape, lambda i, hp: (0, 0))
    band = pltpu.VMEM((2, BLK, 2 * BLK), F32)
    stat = pltpu.VMEM((3, SEQ, LANES), F32)
    return pl.pallas_call(
        _attn_a_kernel,
        grid=(b, n_pairs),
        in_specs=[pl.BlockSpec(memory_space=pltpu.SMEM)] + [col] * 10 + [
            const((BLK, BLK)), const((BLK, 2 * BLK)), const((BLK, 2 * BLK)),
            const((2 * BLK, LANES)), const((4 * BLK, LANES))],
        out_specs=pl.BlockSpec((None, SEQ, LANES), lambda i, hp: (i, 0, hp)),
        out_shape=jax.ShapeDtypeStruct((b, SEQ, D_A), F32),
        scratch_shapes=[pltpu.VMEM((2, BLK, BLK), F32), band, band, band, band,
                        stat, stat, stat, pltpu.VMEM((2, GROUP, BLK, 4 * BLK), BF16)],
        compiler_params=pltpu.CompilerParams(
            dimension_semantics=("arbitrary", "arbitrary"), vmem_limit_bytes=VMEM_LIMIT),
        name="attn_dilated",
    )(slopes_a, *qkv, d16, d4, d1a, e1, e2)


def _attn_b_kernel(slopes_ref, sinks_ref, q1, k1l, k1h, v1l, v1h, d1_ref, e2_ref, o_ref,
                   b1, b1f, p_scr, m_scr):
    hp = pl.program_id(1)
    lo = _lane_lo()
    sink0 = LOG2E * sinks_ref[2 * hp]
    sink1 = LOG2E * sinks_ref[2 * hp + 1]
    b1[0] = (-LOG2E * slopes_ref[2 * hp]) * d1_ref[...]
    b1[1] = (-LOG2E * slopes_ref[2 * hp + 1]) * d1_ref[...]
    _first_block_bias(b1, b1f)
    k1, v1 = (k1l, k1h), (v1l, v1h)

    sink = jnp.where(lo, sink0, sink1)

    def src(blk):
        return [(blk * BLK, BLK)]

    def blocks_of(g):
        return [g * GROUP + i for i in range(GROUP)]

    def first_half(g, slot):
        blocks = blocks_of(g)
        bias0, bias1 = _band_bias(blocks, N_BLK, b1, b1f)
        p, m0, m1 = _scores_softmax(q1, k1, src, blocks, _band_windows(blocks, N_BLK),
                                    bias0, bias1)
        p_scr[slot] = p
        for i in range(GROUP):
            m_scr[slot, i] = jnp.where(lo, m0[i], m1[i])

    def second_half(g, slot):
        blocks = blocks_of(g)
        num, den = _weighted_values(p_scr[slot], v1, e2_ref, src, _band_windows(blocks, N_BLK))
        for i, blk in enumerate(blocks):
            m = m_scr[slot, i]
            m2 = jnp.maximum(m, sink)
            c = jnp.exp2(m - m2)
            out = (num[i] * c) / (den[i] * c + jnp.exp2(sink - m2))
            for r in range(N_RES):
                o_ref[pl.ds(r * N_J + 8 * blk, 8), :] = out[r * 8:(r + 1) * 8]

    _pipelined(N_GROUPS, first_half, second_half)


def _attn_b(q1, kv1, slopes_b, sinks, d1b, e2):
    b = q1.shape[0]
    n_pairs = N_HEADS_B // 2
    pairs_per_kv = n_pairs // N_KV_B
    kv_tile = lambda tile: pl.BlockSpec(
        (None, SEQ, LANES), lambda i, hp, tile=tile: (i, 0, tile + 2 * (hp // pairs_per_kv)))
    band = pltpu.VMEM((2, BLK, 2 * BLK), F32)
    return pl.pallas_call(
        _attn_b_kernel,
        grid=(b, n_pairs),
        in_specs=[pl.BlockSpec(memory_space=pltpu.SMEM),
                  pl.BlockSpec(memory_space=pltpu.SMEM),
                  pl.BlockSpec((None, SEQ, LANES), lambda i, hp: (i, 0, hp)),
                  kv_tile(0), kv_tile(1), kv_tile(4), kv_tile(5),
                  pl.BlockSpec((BLK, 2 * BLK), lambda i, hp: (0, 0)),
                  pl.BlockSpec((4 * BLK, LANES), lambda i, hp: (0, 0))],
        out_specs=pl.BlockSpec((None, SEQ, LANES), lambda i, hp: (i, 0, hp)),
        out_shape=jax.ShapeDtypeStruct((b, SEQ, D_B), F32),
        scratch_shapes=[band, band, pltpu.VMEM((2, GROUP, BLK, 4 * BLK), BF16),
                        pltpu.VMEM((2, GROUP, BLK, LANES), F32)],
        compiler_params=pltpu.CompilerParams(
            dimension_semantics=("arbitrary", "arbitrary"), vmem_limit_bytes=VMEM_LIMIT),
        name="attn_swa",
    )(slopes_b, sinks, q1, kv1, kv1, kv1, kv1, d1b, e2)


def _cast_chunks(src_refs, dst_refs):
    for src, dst in zip(src_refs, dst_refs):
        dst[...] = src[...].astype(BF16)


def _out_proj_kernel(a_ref, b_ref, x_ref, ga_ref, gb_ref, w_ref, gf_ref, *rest):
    n_cast = (len(rest) - 2) // 2
    cast_src, (xo_ref, h_ref), cast_dst = rest[:n_cast], rest[n_cast:n_cast + 2], rest[n_cast + 2:]
    na = _rms(a_ref[...], ga_ref[...]).astype(BF16)
    nb = _rms(b_ref[...], gb_ref[...]).astype(BF16)
    mix = jnp.concatenate([na, nb], axis=1)
    x = x_ref[...] + jnp.dot(mix, w_ref[...], preferred_element_type=F32)
    xo_ref[...] = x
    h_ref[...] = _rms(x, gf_ref[...]).astype(BF16)
    _cast_chunks(cast_src, cast_dst)


def _out_proj(a, b, x, ga, gb, w, gf, nxt):
    t = x.shape[0]
    tm = 512
    steps = t // tm
    row = lambda n: pl.BlockSpec((tm, n), lambda m: (m, 0))
    const = lambda r, c: pl.BlockSpec((r, c), lambda m: (0, 0))
    in_specs = [row(D_A), row(D_B), row(D_MODEL), const(1, D_A), const(1, D_B),
                pl.BlockSpec((D_MIX, D_MODEL), lambda m: (0, 0), pipeline_mode=pl.Buffered(1)),
                const(1, D_MODEL)]
    out_specs = [row(D_MODEL), row(D_MODEL)]
    out_shape = [jax.ShapeDtypeStruct((t, D_MODEL), F32), jax.ShapeDtypeStruct((t, D_MODEL), BF16)]
    args = [a, b, x, ga.reshape(1, D_A), gb.reshape(1, D_B), w, gf.reshape(1, D_MODEL)]
    if nxt is not None:
        *stacked, layer = nxt
        for wf in stacked:
            rows, cols = wf.shape[1] // steps, wf.shape[2]
            in_specs.append(pl.BlockSpec((None, rows, cols), lambda m: (layer, m, 0)))
            out_specs.append(pl.BlockSpec((rows, cols), lambda m: (m, 0)))
            out_shape.append(jax.ShapeDtypeStruct(wf.shape[1:], BF16))
        args += stacked
    return pl.pallas_call(
        _out_proj_kernel,
        grid=(steps,),
        in_specs=in_specs,
        out_specs=out_specs,
        out_shape=out_shape,
        compiler_params=pltpu.CompilerParams(
            dimension_semantics=("arbitrary",), vmem_limit_bytes=VMEM_LIMIT),
        name="out_proj",
    )(*args)


GATE_UP_TM, GATE_UP_TF = 2048, 512
DOWN_TM = 256


def _gate_up_kernel(h_ref, wg_ref, wu_ref, *rest):
    n_cast = (len(rest) - 1) // 2
    cast_src, act_ref, cast_dst = rest[:n_cast], rest[n_cast], rest[n_cast + 1:]
    h = h_ref[...]
    gate = jnp.dot(h, wg_ref[...], preferred_element_type=F32)
    up = jnp.dot(h, wu_ref[...], preferred_element_type=F32)
    act_ref[...] = (jax.nn.silu(gate) * up).astype(BF16)
    _cast_chunks(cast_src, cast_dst)


def _gate_up(h, wg, wu, nxt):
    t = h.shape[0]
    tm, tf = GATE_UP_TM, GATE_UP_TF
    n_m, n_f = t // tm, D_FF // tf
    w_spec = pl.BlockSpec((D_MODEL, tf), lambda m, f: (0, f))
    in_specs = [pl.BlockSpec((tm, D_MODEL), lambda m, f: (m, 0)), w_spec, w_spec]
    out_specs = [pl.BlockSpec((tm, tf), lambda m, f: (m, f))]
    out_shape = [jax.ShapeDtypeStruct((t, D_FF), BF16)]
    args = [h, wg, wu]
    if nxt is not None:
        *stacked, layer = nxt
        chunk = (D_MODEL // n_m, D_FF // n_f)
        for wf in stacked:
            in_specs.append(pl.BlockSpec((None,) + chunk, lambda m, f: (layer, m, f)))
            out_specs.append(pl.BlockSpec(chunk, lambda m, f: (m, f)))
            out_shape.append(jax.ShapeDtypeStruct(wf.shape[1:], BF16))
        args += stacked
    return pl.pallas_call(
        _gate_up_kernel,
        grid=(n_m, n_f),
        in_specs=in_specs,
        out_specs=out_specs,
        out_shape=out_shape,
        compiler_params=pltpu.CompilerParams(
            dimension_semantics=("arbitrary", "arbitrary"), vmem_limit_bytes=VMEM_LIMIT),
        name="ffn_gate_up",
    )(*args)


def _down_kernel(act_ref, wd_ref, x_ref, gn_ref, *rest):
    n_cast = (len(rest) - 2) // 2
    cast_src, (xo_ref, hn_ref), cast_dst = rest[:n_cast], rest[n_cast:n_cast + 2], rest[n_cast + 2:]
    x = x_ref[...] + jnp.dot(act_ref[...], wd_ref[...], preferred_element_type=F32)
    xo_ref[...] = x
    hn_ref[...] = _rms(x, gn_ref[...]).astype(hn_ref.dtype)
    _cast_chunks(cast_src, cast_dst)


def _down(act, wd, x, gn, normed_dtype, nxt):
    t = x.shape[0]
    tm = DOWN_TM
    steps = t // tm
    row = lambda n: pl.BlockSpec((tm, n), lambda m: (m, 0))
    in_specs = [row(D_FF),
                pl.BlockSpec((D_FF, D_MODEL), lambda m: (0, 0), pipeline_mode=pl.Buffered(1)),
                row(D_MODEL), pl.BlockSpec((1, D_MODEL), lambda m: (0, 0))]
    out_specs = [row(D_MODEL), row(D_MODEL)]
    out_shape = [jax.ShapeDtypeStruct((t, D_MODEL), F32),
                 jax.ShapeDtypeStruct((t, D_MODEL), normed_dtype)]
    args = [act, wd, x, gn.reshape(1, D_MODEL)]
    if nxt is not None:
        w_down, layer = nxt
        chunk = (D_FF // steps, D_MODEL)
        in_specs.append(pl.BlockSpec((None,) + chunk, lambda m: (layer, m, 0)))
        out_specs.append(pl.BlockSpec(chunk, lambda m: (m, 0)))
        out_shape.append(jax.ShapeDtypeStruct(w_down.shape[1:], BF16))
        args.append(w_down)
    return pl.pallas_call(
        _down_kernel,
        grid=(steps,),
        in_specs=in_specs,
        out_specs=out_specs,
        out_shape=out_shape,
        compiler_params=pltpu.CompilerParams(
            dimension_semantics=("arbitrary",), vmem_limit_bytes=VMEM_LIMIT),
        name="ffn_down",
    )(*args)


def kernel(x, attn_norm, w_in, sinks, out_norm_a, out_norm_b, w_out,
           ffn_norm, w_gate, w_up, w_down, final_norm):
    b, s_len, d = x.shape
    assert (s_len, d) == (SEQ, D_MODEL)
    depth = w_in.shape[0]

    idx = jnp.arange(N_ALIBI, dtype=F32)
    slopes = jnp.exp2(-8.0 * (idx + 1.0) / N_ALIBI)
    slopes_a, slopes_b = slopes[0::2], slopes[1::2]
    d16, d4, d1a, d1b = (jnp.asarray(p) for p in _dist_patterns())
    e1 = jnp.asarray(_ones_pattern(BLK), BF16)
    e2 = jnp.asarray(_ones_pattern(2 * BLK), BF16)

    wi, wo, wg, wu, wd = (w[0].astype(BF16) for w in (w_in, w_out, w_gate, w_up, w_down))

    xp = _permute(x, _to_residue_major_kernel, "to_residue_major").reshape(b * SEQ, D_MODEL)
    h = _norm(xp, attn_norm[0])
    for l in range(depth):
        last = l + 1 == depth
        *qkv_a, qb1, kvb1 = _in_proj(h, wi, b)
        out_a = _attn_a(qkv_a, slopes_a, d16, d4, d1a, e1, e2).reshape(b * SEQ, D_A)
        out_b = _attn_b(qb1, kvb1, slopes_b, sinks[l], d1b, e2).reshape(b * SEQ, D_B)
        xp, h, *nxt_proj = _out_proj(out_a, out_b, xp, out_norm_a[l], out_norm_b[l], wo,
                                     ffn_norm[l], None if last else (w_in, w_out, l + 1))
        act, *nxt_gu = _gate_up(h, wg, wu, None if last else (w_gate, w_up, l + 1))
        xp, h, *nxt_d = _down(act, wd, xp, final_norm if last else attn_norm[l + 1],
                              F32 if last else BF16, None if last else (w_down, l + 1))
        if not last:
            (wi, wo), (wg, wu), (wd,) = nxt_proj, nxt_gu, nxt_d
    return _permute(h.reshape(b, SEQ, D_MODEL), _from_residue_major_kernel, "from_residue_major")
```

```python
import math

import numpy as np
import jax
import jax.numpy as jnp
from jax import lax
from jax.experimental import pallas as pl
from jax.experimental.pallas import tpu as pltpu

F32 = jnp.float32
BF16 = jnp.bfloat16

D_MODEL = 2048
SEQ = 2048
HEAD_DIM = 64
N_HEADS_A = 16
N_HEADS_B = 16
N_KV_B = 2
D_A = N_HEADS_A * HEAD_DIM
D_B = N_HEADS_B * HEAD_DIM
D_KV_B = N_KV_B * HEAD_DIM
D_MIX = D_A + D_B
IN_COLS = 3 * D_A + D_B + 2 * D_KV_B
N_ALIBI = N_HEADS_A + N_HEADS_B
D_FF = 5632
EPS = 1e-6
SWA_WINDOW = 128
DIL_MAX_DIST = 128

LANES = 128
BLK = 128
N_RES = 16
N_J = SEQ // N_RES
N_BLK = SEQ // BLK
GROUP = 4
N_GROUPS = N_BLK // GROUP
LOG2E = math.log2(math.e)
VMEM_LIMIT = 56 * 1024 * 1024


def _rms(x, g):
    return x * lax.rsqrt(jnp.mean(x * x, axis=-1, keepdims=True) + EPS) * g


def _dist_patterns():
    inf = np.float32(np.inf)
    j = np.arange(BLK)
    d16 = (j[:, None] - j[None, :]).astype(np.float32)
    d16 = np.where(d16 >= 0, 16.0 * d16, inf).astype(np.float32)

    t = np.arange(BLK) // 32
    jj = np.arange(BLK) % 32
    sub = 4 * jj + t
    cur = sub[:, None] - sub[None, :]
    prev = cur + BLK
    d4 = np.concatenate([prev, cur], axis=1).astype(np.float32)
    d4 = np.where((d4 >= 0) & (d4 <= DIL_MAX_DIST), 4.0 * d4, inf).astype(np.float32)

    r = np.arange(BLK) // 8
    jj = np.arange(BLK) % 8
    pos = 16 * jj + r
    cur = pos[:, None] - pos[None, :]
    prev = cur + BLK
    d1 = np.concatenate([prev, cur], axis=1).astype(np.float32)
    d1a = np.where((d1 >= 0) & (d1 <= DIL_MAX_DIST), d1, inf).astype(np.float32)
    d1b = np.where((d1 >= 0) & (d1 <= SWA_WINDOW - 1), d1, inf).astype(np.float32)
    return d16, d4, d1a, d1b


def _ones_pattern(nk):
    e = np.zeros((2 * nk, LANES), np.float32)
    e[:nk, :HEAD_DIM] = 1.0
    e[nk:, HEAD_DIM:] = 1.0
    return e


def _to_residue_major_kernel(x_ref, o_ref):
    for r in range(N_RES):
        o_ref[r * N_J:(r + 1) * N_J, :] = x_ref[pl.ds(r, N_J, stride=N_RES), :]


def _from_residue_major_kernel(x_ref, o_ref):
    for r in range(N_RES):
        o_ref[pl.ds(r, N_J, stride=N_RES), :] = x_ref[r * N_J:(r + 1) * N_J, :]


def _permute(x, body, name):
    b = x.shape[0]
    slab = pl.BlockSpec((None, SEQ, LANES), lambda i, c: (i, 0, c))
    return pl.pallas_call(
        body,
        grid=(b, D_MODEL // LANES),
        in_specs=[slab],
        out_specs=slab,
        out_shape=jax.ShapeDtypeStruct(x.shape, F32),
        compiler_params=pltpu.CompilerParams(
            dimension_semantics=("arbitrary", "arbitrary"), vmem_limit_bytes=VMEM_LIMIT),
        name=name,
    )(x)


def _norm_kernel(x_ref, g_ref, h_ref):
    h_ref[...] = _rms(x_ref[...], g_ref[...]).astype(BF16)


def _norm(x, g):
    t = x.shape[0]
    tm = 512
    return pl.pallas_call(
        _norm_kernel,
        grid=(t // tm,),
        in_specs=[pl.BlockSpec((tm, D_MODEL), lambda m: (m, 0)),
                  pl.BlockSpec((1, D_MODEL), lambda m: (0, 0))],
        out_specs=pl.BlockSpec((tm, D_MODEL), lambda m: (m, 0)),
        out_shape=jax.ShapeDtypeStruct((t, D_MODEL), BF16),
        compiler_params=pltpu.CompilerParams(
            dimension_semantics=("arbitrary",), vmem_limit_bytes=VMEM_LIMIT),
        name="first_norm",
    )(x, g.reshape(1, D_MODEL))


IN_PROJ_RES = 4
IN_PROJ_TM = IN_PROJ_RES * N_J
IN_PROJ_STEPS = N_RES // IN_PROJ_RES
Q_SCALE = LOG2E * HEAD_DIM ** -0.5


def _in_proj_kernel(h_ref, w_ref, q16, k16l, k16h, v16l, v16h, q1, k1l, k1h, v1l, v1h,
                    qb1, kvb1):
    h = h_ref[...]

    def order1(x):
        n = x.shape[1]
        pieces = [x[r * N_J + jb * 8:r * N_J + jb * 8 + 8]
                  for jb in range(N_J // 8) for r in range(IN_PROJ_RES)]
        return jnp.concatenate(pieces, axis=0).reshape(N_J // 8, 8 * IN_PROJ_RES, n)

    def proj(col, width, scale=None):
        x = jnp.dot(h, w_ref[:, col:col + width], preferred_element_type=F32)
        return x if scale is None else x * scale

    def store2(x, o16, o1):
        o16[...] = x.astype(BF16)
        o1[...] = order1(x).astype(BF16)

    def first_head(width):
        return (lax.broadcasted_iota(jnp.int32, (IN_PROJ_TM, width), 1) & HEAD_DIM) == 0

    store2(proj(0, D_A, Q_SCALE), q16, q1)
    lo = first_head(D_A)
    for i, outs in ((1, ((k16l, k1l), (k16h, k1h))), (2, ((v16l, v1l), (v16h, v1h)))):
        x = proj(i * D_A, D_A)
        store2(jnp.where(lo, x, 0.0), *outs[0])
        store2(jnp.where(lo, 0.0, x), *outs[1])
    qb1[...] = order1(proj(3 * D_A, D_B, Q_SCALE)).astype(BF16)

    x = proj(3 * D_A + D_B, 2 * D_KV_B)
    lo = first_head(LANES)
    tiles = []
    for part in (x[:, :LANES], x[:, LANES:]):
        swapped = pltpu.roll(part, HEAD_DIM, 1)
        tiles += [jnp.where(lo, part, 0.0), jnp.where(lo, 0.0, swapped),
                  jnp.where(lo, swapped, 0.0), jnp.where(lo, 0.0, part)]
    kvb1[...] = order1(jnp.concatenate(tiles, axis=1)).astype(BF16)


def _in_proj(h, w, b):
    t = h.shape[0]
    per = IN_PROJ_STEPS
    o16 = pl.BlockSpec((IN_PROJ_TM, D_A), lambda m: (m, 0))
    o1 = lambda c: pl.BlockSpec((None, N_J // 8, None, 8 * IN_PROJ_RES, c),
                                lambda m: (m // per, 0, m % per, 0, 0))
    s16 = jax.ShapeDtypeStruct((t, D_A), BF16)
    s1 = lambda c: jax.ShapeDtypeStruct((b, N_J // 8, per, 8 * IN_PROJ_RES, c), BF16)
    outs = pl.pallas_call(
        _in_proj_kernel,
        grid=(t // IN_PROJ_TM,),
        in_specs=[pl.BlockSpec((IN_PROJ_TM, D_MODEL), lambda m: (m, 0)),
                  pl.BlockSpec((D_MODEL, IN_COLS), lambda m: (0, 0),
                               pipeline_mode=pl.Buffered(1))],
        out_specs=[o16] * 5 + [o1(D_A)] * 5 + [o1(D_B), o1(8 * LANES)],
        out_shape=[s16] * 5 + [s1(D_A)] * 5 + [s1(D_B), s1(8 * LANES)],
        compiler_params=pltpu.CompilerParams(
            dimension_semantics=("arbitrary",), vmem_limit_bytes=VMEM_LIMIT),
        name="in_proj",
    )(h, w)
    return [o.reshape(b, SEQ, o.shape[-1]) for o in outs]


def _lane_lo():
    return lax.broadcasted_iota(jnp.int32, (BLK, LANES), 1) < HEAD_DIM


def _rows(ref, chunks):
    return jnp.concatenate([ref[pl.ds(start, size), :] for start, size in chunks], axis=0)


def _window(pair, window, src):
    chunks = [c for blk in window for c in src(blk)]
    return jnp.concatenate([_rows(ref, chunks) for ref in pair], axis=0)


def _scores_softmax(qs, ks, src, blocks, windows, bias0, bias1):
    nk = BLK * len(windows[0])
    q = jnp.stack([_rows(qs, src(blk)) for blk in blocks])
    kk = jnp.stack([_window(ks, w, src) for w in windows])
    s = jnp.einsum("gqd,gkd->gqk", q, kk, preferred_element_type=F32)
    l0 = s[:, :, :nk] + bias0
    l1 = s[:, :, nk:] + bias1
    m0 = jnp.max(l0, axis=2, keepdims=True)
    m1 = jnp.max(l1, axis=2, keepdims=True)
    p = jnp.concatenate([jnp.exp2(l0 - m0), jnp.exp2(l1 - m1)], axis=2).astype(BF16)
    return p, m0, m1


def _weighted_values(p, vs, ones_ref, src, windows):
    ones = ones_ref[...]
    vv = jnp.stack([jnp.concatenate([_window(vs, w, src), ones], axis=1) for w in windows])
    na = jnp.einsum("gqk,gkd->gqd", p, vv, preferred_element_type=F32)
    return na[:, :, :LANES], na[:, :, LANES:]


def _is_first(blk, per):
    return per is not None and blk % per == 0


def _band_windows(blocks, per):
    if per is None:
        return [(blk,) for blk in blocks]
    return [(blk, blk + 1) if _is_first(blk, per) else (blk - 1, blk) for blk in blocks]


def _band_bias(blocks, per, bias, bias_first):
    if not any(_is_first(blk, per) for blk in blocks):
        return bias[0], bias[1]
    return tuple(jnp.stack([(bias_first if _is_first(blk, per) else bias)[h] for blk in blocks])
                 for h in range(2))


def _pipelined(n_groups, first_half, second_half):
    first_half(0, 0)
    for g in range(n_groups - 1):
        first_half(g + 1, (g + 1) % 2)
        second_half(g, g % 2)
    second_half(n_groups - 1, (n_groups - 1) % 2)


def _first_block_bias(bias, bias_first):
    for h in range(2):
        bias_first[h, :, :BLK] = bias[h, :, BLK:]
        bias_first[h, :, BLK:] = jnp.full((BLK, BLK), -jnp.inf, F32)


def _dilated_branches(slopes_ref, hp, lo, q16, k16, v16, q1, k1, v1, dists, ones, biases,
                      num_s, m_s, den_s, p_scr):
    d16_ref, d4_ref, d1_ref = dists
    e1_ref, e2_ref = ones
    b16, b4, b4f, b1, b1f = biases
    ns0 = -LOG2E * slopes_ref[2 * hp]
    ns1 = -LOG2E * slopes_ref[2 * hp + 1]
    for bias, dist in ((b16, d16_ref), (b4, d4_ref), (b1, d1_ref)):
        bias[0] = ns0 * dist[...]
        bias[1] = ns1 * dist[...]
    _first_block_bias(b4, b4f)
    _first_block_bias(b1, b1f)

    def chunks16(blk):
        return [(blk * BLK, BLK)]

    def chunks4(blk):
        r4, mb = divmod(blk, 4)
        return [((r4 + 4 * t) * N_J + 32 * mb, 32) for t in range(4)]

    def chunks1_residue_major(blk):
        return [(r * N_J + 8 * blk, 8) for r in range(N_RES)]

    def scatter(dst_scr, branch, x, chunks):
        row = 0
        for start, size in chunks:
            dst_scr[branch, pl.ds(start, size), :] = x[row:row + size]
            row += size

    def blocks_of(g):
        return [g * GROUP + i for i in range(GROUP)]

    def branch_halves(qs, ks, vs, ones_ref, src, dst, bias, bias_first, per, branch):
        def first_half(g, slot):
            blocks = blocks_of(g)
            windows = _band_windows(blocks, per)
            bias0, bias1 = _band_bias(blocks, per, bias, bias_first)
            p, m0, m1 = _scores_softmax(qs, ks, src, blocks, windows, bias0, bias1)
            p_scr[slot, :, :, :p.shape[2]] = p
            for i, blk in enumerate(blocks):
                scatter(m_s, branch, jnp.where(lo, m0[i], m1[i]), dst(blk))

        def second_half(g, slot):
            blocks = blocks_of(g)
            windows = _band_windows(blocks, per)
            width = 2 * BLK * len(windows[0])
            num, den = _weighted_values(p_scr[slot, :, :, :width], vs, ones_ref, src, windows)
            for i, blk in enumerate(blocks):
                scatter(num_s, branch, num[i], dst(blk))
                scatter(den_s, branch, den[i], dst(blk))

        return first_half, second_half

    _pipelined(N_GROUPS, *branch_halves(q16, k16, v16, e1_ref, chunks16, chunks16,
                                        b16, None, None, 2))
    _pipelined(N_GROUPS, *branch_halves(q16, k16, v16, e2_ref, chunks4, chunks4,
                                        b4, b4f, 4, 1))
    _pipelined(N_GROUPS, *branch_halves(q1, k1, v1, e2_ref, chunks16, chunks1_residue_major,
                                        b1, b1f, N_BLK, 0))


def _sliding_window(slopes_ref, sinks_ref, hp, lo, q1, k1, v1, d1_ref, e2_ref, b1, b1f,
                    o_scr, p_scr, m_scr):
    sink = jnp.where(lo, LOG2E * sinks_ref[2 * hp], LOG2E * sinks_ref[2 * hp + 1])
    b1[0] = (-LOG2E * slopes_ref[2 * hp]) * d1_ref[...]
    b1[1] = (-LOG2E * slopes_ref[2 * hp + 1]) * d1_ref[...]
    _first_block_bias(b1, b1f)

    def src(blk):
        return [(blk * BLK, BLK)]

    def blocks_of(g):
        return [g * GROUP + i for i in range(GROUP)]

    def first_half(g, slot):
        blocks = blocks_of(g)
        bias0, bias1 = _band_bias(blocks, N_BLK, b1, b1f)
        p, m0, m1 = _scores_softmax(q1, k1, src, blocks, _band_windows(blocks, N_BLK),
                                    bias0, bias1)
        p_scr[slot] = p
        for i in range(GROUP):
            m_scr[slot, i] = jnp.where(lo, m0[i], m1[i])

    def second_half(g, slot):
        blocks = blocks_of(g)
        num, den = _weighted_values(p_scr[slot], v1, e2_ref, src, _band_windows(blocks, N_BLK))
        for i, blk in enumerate(blocks):
            m = m_scr[slot, i]
            m2 = jnp.maximum(m, sink)
            c = jnp.exp2(m - m2)
            out = (num[i] * c) / (den[i] * c + jnp.exp2(sink - m2))
            for r in range(N_RES):
                o_scr[pl.ds(r * N_J + 8 * blk, 8), :] = out[r * 8:(r + 1) * 8]

    _pipelined(N_GROUPS, first_half, second_half)


def _attn_kernel(slopes_a_ref, slopes_b_ref, sinks_ref,
                 q16, k16l, k16h, v16l, v16h, q1, k1l, k1h, v1l, v1h,
                 qb, kbl, kbh, vbl, vbh, d16_ref, d4_ref, d1a_ref, d1b_ref, e1_ref, e2_ref,
                 oa_ref, ob_ref,
                 b16, b4, b4f, b1, b1f, bb, bbf, num_s, m_s, den_s, pa_scr, pb_scr, mb_scr, ob_scr):
    hp = pl.program_id(1)
    lo = _lane_lo()
    _dilated_branches(slopes_a_ref, hp, lo, q16, (k16l, k16h), (v16l, v16h),
                      q1, (k1l, k1h), (v1l, v1h), (d16_ref, d4_ref, d1a_ref),
                      (e1_ref, e2_ref), (b16, b4, b4f, b1, b1f), num_s, m_s, den_s, pa_scr)
    _sliding_window(slopes_b_ref, sinks_ref, hp, lo, qb, (kbl, kbh), (vbl, vbh),
                    d1b_ref, e2_ref, bb, bbf, ob_scr, pb_scr, mb_scr)

    def finish(g, carry):
        rows = pl.ds(pl.multiple_of(g * BLK, BLK), BLK)
        m = [m_s[i, rows, :] for i in range(3)]
        m_all = jnp.maximum(jnp.maximum(m[0], m[1]), m[2])
        w = [jnp.exp2(mi - m_all) for mi in m]
        numer = num_s[0, rows, :] * w[0] + num_s[1, rows, :] * w[1] + num_s[2, rows, :] * w[2]
        denom = den_s[0, rows, :] * w[0] + den_s[1, rows, :] * w[1] + den_s[2, rows, :] * w[2]
        oa_ref[rows, :] = (numer / denom).astype(oa_ref.dtype)
        ob_ref[rows, :] = ob_scr[rows, :].astype(ob_ref.dtype)
        return carry

    lax.fori_loop(0, N_BLK, finish, 0, unroll=2)


def _attention(qkv_a, qb1, kvb1, slopes_a, slopes_b, sinks, d16, d4, d1a, d1b, e1, e2):
    b = qb1.shape[0]
    n_pairs = N_HEADS_A // 2
    assert N_HEADS_B // 2 == n_pairs
    pairs_per_kv = n_pairs // N_KV_B
    smem = pl.BlockSpec(memory_space=pltpu.SMEM)
    col = pl.BlockSpec((None, SEQ, LANES), lambda i, hp: (i, 0, hp))
    kv_tile = lambda tile: pl.BlockSpec(
        (None, SEQ, LANES), lambda i, hp, tile=tile: (i, 0, tile + 2 * (hp // pairs_per_kv)))
    const = lambda shape: pl.BlockSpec(shape, lambda i, hp: (0, 0))
    band = pltpu.VMEM((2, BLK, 2 * BLK), F32)
    stat = pltpu.VMEM((3, SEQ, LANES), F32)
    probs = pltpu.VMEM((2, GROUP, BLK, 4 * BLK), BF16)
    return pl.pallas_call(
        _attn_kernel,
        grid=(b, n_pairs),
        in_specs=[smem, smem, smem] + [col] * 11 + [
            kv_tile(0), kv_tile(1), kv_tile(4), kv_tile(5),
            const((BLK, BLK)), const((BLK, 2 * BLK)), const((BLK, 2 * BLK)),
            const((BLK, 2 * BLK)), const((2 * BLK, LANES)), const((4 * BLK, LANES))],
        out_specs=[col, col],
        out_shape=[jax.ShapeDtypeStruct((b, SEQ, D_A), BF16),
                   jax.ShapeDtypeStruct((b, SEQ, D_B), BF16)],
        scratch_shapes=[pltpu.VMEM((2, BLK, BLK), F32), band, band, band, band, band, band,
                        stat, stat, stat, probs, probs,
                        pltpu.VMEM((2, GROUP, BLK, LANES), F32),
                        pltpu.VMEM((SEQ, LANES), F32)],
        compiler_params=pltpu.CompilerParams(
            dimension_semantics=("arbitrary", "arbitrary"), vmem_limit_bytes=VMEM_LIMIT),
        name="attention",
    )(slopes_a, slopes_b, sinks, *qkv_a, qb1, kvb1, kvb1, kvb1, kvb1, d16, d4, d1a, d1b, e1, e2)


def _cast_chunks(src_refs, dst_refs):
    for src, dst in zip(src_refs, dst_refs):
        dst[...] = src[...].astype(BF16)


def _out_proj_kernel(a_ref, b_ref, x_ref, ga_ref, gb_ref, w_ref, gf_ref, *rest):
    n_cast = (len(rest) - 2) // 2
    cast_src, (xo_ref, h_ref), cast_dst = rest[:n_cast], rest[n_cast:n_cast + 2], rest[n_cast + 2:]
    na = _rms(a_ref[...].astype(F32), ga_ref[...]).astype(BF16)
    nb = _rms(b_ref[...].astype(F32), gb_ref[...]).astype(BF16)
    mix = jnp.concatenate([na, nb], axis=1)
    x = x_ref[...] + jnp.dot(mix, w_ref[...], preferred_element_type=F32)
    xo_ref[...] = x
    h_ref[...] = _rms(x, gf_ref[...]).astype(BF16)
    _cast_chunks(cast_src, cast_dst)


def _out_proj(a, b, x, ga, gb, w, gf, nxt):
    t = x.shape[0]
    tm = 512
    steps = t // tm
    row = lambda n: pl.BlockSpec((tm, n), lambda m: (m, 0))
    const = lambda r, c: pl.BlockSpec((r, c), lambda m: (0, 0))
    in_specs = [row(D_A), row(D_B), row(D_MODEL), const(1, D_A), const(1, D_B),
                pl.BlockSpec((D_MIX, D_MODEL), lambda m: (0, 0), pipeline_mode=pl.Buffered(1)),
                const(1, D_MODEL)]
    out_specs = [row(D_MODEL), row(D_MODEL)]
    out_shape = [jax.ShapeDtypeStruct((t, D_MODEL), F32), jax.ShapeDtypeStruct((t, D_MODEL), BF16)]
    args = [a, b, x, ga.reshape(1, D_A), gb.reshape(1, D_B), w, gf.reshape(1, D_MODEL)]
    if nxt is not None:
        *stacked, layer = nxt
        for wf in stacked:
            rows, cols = wf.shape[1] // steps, wf.shape[2]
            in_specs.append(pl.BlockSpec((None, rows, cols), lambda m: (layer, m, 0)))
            out_specs.append(pl.BlockSpec((rows, cols), lambda m: (m, 0)))
            out_shape.append(jax.ShapeDtypeStruct(wf.shape[1:], BF16))
        args += stacked
    return pl.pallas_call(
        _out_proj_kernel,
        grid=(steps,),
        in_specs=in_specs,
        out_specs=out_specs,
        out_shape=out_shape,
        compiler_params=pltpu.CompilerParams(
            dimension_semantics=("arbitrary",), vmem_limit_bytes=VMEM_LIMIT),
        name="out_proj",
    )(*args)


GATE_UP_TM, GATE_UP_TF = 2048, 512
DOWN_TM = 256


def _gate_up_kernel(h_ref, wg_ref, wu_ref, *rest):
    n_cast = (len(rest) - 1) // 2
    cast_src, act_ref, cast_dst = rest[:n_cast], rest[n_cast], rest[n_cast + 1:]
    h = h_ref[...]
    gate = jnp.dot(h, wg_ref[...], preferred_element_type=F32)
    up = jnp.dot(h, wu_ref[...], preferred_element_type=F32)
    act_ref[...] = (jax.nn.silu(gate) * up).astype(BF16)
    _cast_chunks(cast_src, cast_dst)


def _gate_up(h, wg, wu, nxt):
    t = h.shape[0]
    tm, tf = GATE_UP_TM, GATE_UP_TF
    n_m, n_f = t // tm, D_FF // tf
    w_spec = pl.BlockSpec((D_MODEL, tf), lambda m, f: (0, f))
    in_specs = [pl.BlockSpec((tm, D_MODEL), lambda m, f: (m, 0)), w_spec, w_spec]
    out_specs = [pl.BlockSpec((tm, tf), lambda m, f: (m, f))]
    out_shape = [jax.ShapeDtypeStruct((t, D_FF), BF16)]
    args = [h, wg, wu]
    if nxt is not None:
        *stacked, layer = nxt
        chunk = (D_MODEL // n_m, D_FF // n_f)
        for wf in stacked:
            in_specs.append(pl.BlockSpec((None,) + chunk, lambda m, f: (layer, m, f)))
            out_specs.append(pl.BlockSpec(chunk, lambda m, f: (m, f)))
            out_shape.append(jax.ShapeDtypeStruct(wf.shape[1:], BF16))
        args += stacked
    return pl.pallas_call(
        _gate_up_kernel,
        grid=(n_m, n_f),
        in_specs=in_specs,
        out_specs=out_specs,
        out_shape=out_shape,
        compiler_params=pltpu.CompilerParams(
            dimension_semantics=("arbitrary", "arbitrary"), vmem_limit_bytes=VMEM_LIMIT),
        name="ffn_gate_up",
    )(*args)


def _down_kernel(act_ref, wd_ref, x_ref, gn_ref, *rest):
    n_cast = (len(rest) - 2) // 2
    cast_src, (xo_ref, hn_ref), cast_dst = rest[:n_cast], rest[n_cast:n_cast + 2], rest[n_cast + 2:]
    x = x_ref[...] + jnp.dot(act_ref[...], wd_ref[...], preferred_element_type=F32)
    xo_ref[...] = x
    hn_ref[...] = _rms(x, gn_ref[...]).astype(hn_ref.dtype)
    _cast_chunks(cast_src, cast_dst)


def _down(act, wd, x, gn, normed_dtype, nxt):
    t = x.shape[0]
    tm = DOWN_TM
    steps = t // tm
    row = lambda n: pl.BlockSpec((tm, n), lambda m: (m, 0))
    in_specs = [row(D_FF),
                pl.BlockSpec((D_FF, D_MODEL), lambda m: (0, 0), pipeline_mode=pl.Buffered(1)),
                row(D_MODEL), pl.BlockSpec((1, D_MODEL), lambda m: (0, 0))]
    out_specs = [row(D_MODEL), row(D_MODEL)]
    out_shape = [jax.ShapeDtypeStruct((t, D_MODEL), F32),
                 jax.ShapeDtypeStruct((t, D_MODEL), normed_dtype)]
    args = [act, wd, x, gn.reshape(1, D_MODEL)]
    if nxt is not None:
        w_down, layer = nxt
        chunk = (D_FF // steps, D_MODEL)
        in_specs.append(pl.BlockSpec((None,) + chunk, lambda m: (layer, m, 0)))
        out_specs.append(pl.BlockSpec(chunk, lambda m: (m, 0)))
        out_shape.append(jax.ShapeDtypeStruct(w_down.shape[1:], BF16))
        args.append(w_down)
    return pl.pallas_call(
        _down_kernel,
        grid=(steps,),
        in_specs=in_specs,
        out_specs=out_specs,
        out_shape=out_shape,
        compiler_params=pltpu.CompilerParams(
            dimension_semantics=("arbitrary",), vmem_limit_bytes=VMEM_LIMIT),
        name="ffn_down",
    )(*args)


def kernel(x, attn_norm, w_in, sinks, out_norm_a, out_norm_b, w_out,
           ffn_norm, w_gate, w_up, w_down, final_norm):
    b, s_len, d = x.shape
    assert (s_len, d) == (SEQ, D_MODEL)
    depth = w_in.shape[0]

    idx = jnp.arange(N_ALIBI, dtype=F32)
    slopes = jnp.exp2(-8.0 * (idx + 1.0) / N_ALIBI)
    slopes_a, slopes_b = slopes[0::2], slopes[1::2]
    d16, d4, d1a, d1b = (jnp.asarray(p) for p in _dist_patterns())
    e1 = jnp.asarray(_ones_pattern(BLK), BF16)
    e2 = jnp.asarray(_ones_pattern(2 * BLK), BF16)

    wi, wo, wg, wu, wd = (w[0].astype(BF16) for w in (w_in, w_out, w_gate, w_up, w_down))

    xp = _permute(x, _to_residue_major_kernel, "to_residue_major").reshape(b * SEQ, D_MODEL)
    h = _norm(xp, attn_norm[0])
    for l in range(depth):
        last = l + 1 == depth
        *qkv_a, qb1, kvb1 = _in_proj(h, wi, b)
        out_a, out_b = _attention(qkv_a, qb1, kvb1, slopes_a, slopes_b, sinks[l],
                                  d16, d4, d1a, d1b, e1, e2)
        out_a, out_b = out_a.reshape(b * SEQ, D_A), out_b.reshape(b * SEQ, D_B)
        xp, h, *nxt_proj = _out_proj(out_a, out_b, xp, out_norm_a[l], out_norm_b[l], wo,
                                     ffn_norm[l], None if last else (w_in, w_out, l + 1))
        act, *nxt_gu = _gate_up(h, wg, wu, None if last else (w_gate, w_up, l + 1))
        xp, h, *nxt_d = _down(act, wd, xp, final_norm if last else attn_norm[l + 1],
                              F32 if last else BF16, None if last else (w_down, l + 1))
        if not last:
            (wi, wo), (wg, wu), (wd,) = nxt_proj, nxt_gu, nxt_d
    return _permute(h.reshape(b, SEQ, D_MODEL), _from_residue_major_kernel, "from_residue_major")
```

```python
import math

import numpy as np
import jax
import jax.numpy as jnp
from jax import lax
from jax.experimental import pallas as pl
from jax.experimental.pallas import tpu as pltpu

F32 = jnp.float32
BF16 = jnp.bfloat16

D_MODEL = 2048
SEQ = 2048
HEAD_DIM = 64
N_HEADS_A = 16
N_HEADS_B = 16
N_KV_B = 2
D_A = N_HEADS_A * HEAD_DIM
D_B = N_HEADS_B * HEAD_DIM
D_KV_B = N_KV_B * HEAD_DIM
D_MIX = D_A + D_B
IN_COLS = 3 * D_A + D_B + 2 * D_KV_B
N_ALIBI = N_HEADS_A + N_HEADS_B
D_FF = 5632
EPS = 1e-6
SWA_WINDOW = 128
DIL_MAX_DIST = 128

LANES = 128
BLK = 128
N_RES = 16
N_J = SEQ // N_RES
N_BLK = SEQ // BLK
GROUP = 4
N_GROUPS = N_BLK // GROUP
LOG2E = math.log2(math.e)
VMEM_LIMIT = 56 * 1024 * 1024


def _rms(x, g):
    return x * lax.rsqrt(jnp.mean(x * x, axis=-1, keepdims=True) + EPS) * g


def _dist_patterns():
    inf = np.float32(np.inf)
    j = np.arange(BLK)
    d16 = (j[:, None] - j[None, :]).astype(np.float32)
    d16 = np.where(d16 >= 0, 16.0 * d16, inf).astype(np.float32)

    t = np.arange(BLK) // 32
    jj = np.arange(BLK) % 32
    sub = 4 * jj + t
    cur = sub[:, None] - sub[None, :]
    prev = cur + BLK
    d4 = np.concatenate([prev, cur], axis=1).astype(np.float32)
    d4 = np.where((d4 >= 0) & (d4 <= DIL_MAX_DIST), 4.0 * d4, inf).astype(np.float32)

    r = np.arange(BLK) // 8
    jj = np.arange(BLK) % 8
    pos = 16 * jj + r
    cur = pos[:, None] - pos[None, :]
    prev = cur + BLK
    d1 = np.concatenate([prev, cur], axis=1).astype(np.float32)
    d1a = np.where((d1 >= 0) & (d1 <= DIL_MAX_DIST), d1, inf).astype(np.float32)
    d1b = np.where((d1 >= 0) & (d1 <= SWA_WINDOW - 1), d1, inf).astype(np.float32)
    return d16, d4, d1a, d1b


def _ones_pattern(nk):
    e = np.zeros((2 * nk, LANES), np.float32)
    e[:nk, :HEAD_DIM] = 1.0
    e[nk:, HEAD_DIM:] = 1.0
    return e


PERM_J, PERM_R = 64, 8


def _permute_norm_kernel(x_ref, g_ref, xp_ref, h_ref):
    x = jnp.swapaxes(x_ref[...], 0, 1)
    xp_ref[...] = x
    h_ref[...] = _rms(x, g_ref[...]).astype(BF16)


def _permute_norm(x, g):
    b = x.shape[0]
    nat = pl.BlockSpec((None, PERM_J, PERM_R, D_MODEL), lambda i, r, j: (i, j, r, 0))
    res = pl.BlockSpec((None, PERM_R, PERM_J, D_MODEL), lambda i, r, j: (i, r, j, 0))
    xp, h = pl.pallas_call(
        _permute_norm_kernel,
        grid=(b, N_RES // PERM_R, N_J // PERM_J),
        in_specs=[nat, pl.BlockSpec((1, 1, D_MODEL), lambda i, r, j: (0, 0, 0))],
        out_specs=[res, res],
        out_shape=[jax.ShapeDtypeStruct((b, N_RES, N_J, D_MODEL), F32),
                   jax.ShapeDtypeStruct((b, N_RES, N_J, D_MODEL), BF16)],
        compiler_params=pltpu.CompilerParams(
            dimension_semantics=("arbitrary",) * 3, vmem_limit_bytes=VMEM_LIMIT),
        name="permute_norm",
    )(x.reshape(b, N_J, N_RES, D_MODEL), g.reshape(1, 1, D_MODEL))
    return xp.reshape(b * SEQ, D_MODEL), h.reshape(b * SEQ, D_MODEL)


def _unpermute_kernel(y_ref, o_ref):
    o_ref[...] = jnp.swapaxes(y_ref[...], 0, 1)


def _unpermute(y, b):
    nat = pl.BlockSpec((None, PERM_J, PERM_R, D_MODEL), lambda i, r, j: (i, j, r, 0))
    res = pl.BlockSpec((None, PERM_R, PERM_J, D_MODEL), lambda i, r, j: (i, r, j, 0))
    out = pl.pallas_call(
        _unpermute_kernel,
        grid=(b, N_RES // PERM_R, N_J // PERM_J),
        in_specs=[res],
        out_specs=nat,
        out_shape=jax.ShapeDtypeStruct((b, N_J, N_RES, D_MODEL), F32),
        compiler_params=pltpu.CompilerParams(
            dimension_semantics=("arbitrary",) * 3, vmem_limit_bytes=VMEM_LIMIT),
        name="unpermute",
    )(y.reshape(b, N_RES, N_J, D_MODEL))
    return out.reshape(b, SEQ, D_MODEL)


IN_PROJ_RES = 4
IN_PROJ_TM = IN_PROJ_RES * N_J
IN_PROJ_STEPS = N_RES // IN_PROJ_RES
Q_SCALE = LOG2E * HEAD_DIM ** -0.5


N_IN_PROJ_OUT = 12


def _in_proj_kernel(h_ref, w_ref, *rest):
    n_cast = (len(rest) - N_IN_PROJ_OUT) // 2
    cast_src, cast_dst = rest[:n_cast], rest[n_cast + N_IN_PROJ_OUT:]
    q16, k16l, k16h, v16l, v16h, q1, k1l, k1h, v1l, v1h, qb1, kvb1 = rest[n_cast:n_cast + N_IN_PROJ_OUT]
    h = h_ref[...]

    def order1(x):
        n = x.shape[1]
        pieces = [x[r * N_J + jb * 8:r * N_J + jb * 8 + 8]
                  for jb in range(N_J // 8) for r in range(IN_PROJ_RES)]
        return jnp.concatenate(pieces, axis=0).reshape(N_J // 8, 8 * IN_PROJ_RES, n)

    def proj(col, width, scale=None):
        x = jnp.dot(h, w_ref[:, col:col + width], preferred_element_type=F32)
        return x if scale is None else x * scale

    def store2(x, o16, o1):
        o16[...] = x.astype(BF16)
        o1[...] = order1(x).astype(BF16)

    def first_head(width):
        return (lax.broadcasted_iota(jnp.int32, (IN_PROJ_TM, width), 1) & HEAD_DIM) == 0

    store2(proj(0, D_A, Q_SCALE), q16, q1)
    lo = first_head(D_A)
    for i, outs in ((1, ((k16l, k1l), (k16h, k1h))), (2, ((v16l, v1l), (v16h, v1h)))):
        x = proj(i * D_A, D_A)
        store2(jnp.where(lo, x, 0.0), *outs[0])
        store2(jnp.where(lo, 0.0, x), *outs[1])
    qb1[...] = order1(proj(3 * D_A, D_B, Q_SCALE)).astype(BF16)

    x = proj(3 * D_A + D_B, 2 * D_KV_B)
    lo = first_head(LANES)
    tiles = []
    for part in (x[:, :LANES], x[:, LANES:]):
        swapped = pltpu.roll(part, HEAD_DIM, 1)
        tiles += [jnp.where(lo, part, 0.0), jnp.where(lo, 0.0, swapped),
                  jnp.where(lo, swapped, 0.0), jnp.where(lo, 0.0, part)]
    kvb1[...] = order1(jnp.concatenate(tiles, axis=1)).astype(BF16)
    _cast_chunks(cast_src, cast_dst)


def _in_proj(h, w, b, nxt):
    t = h.shape[0]
    steps = t // IN_PROJ_TM
    per = IN_PROJ_STEPS
    o16 = pl.BlockSpec((IN_PROJ_TM, D_A), lambda m: (m, 0))
    o1 = lambda c: pl.BlockSpec((None, N_J // 8, None, 8 * IN_PROJ_RES, c),
                                lambda m: (m // per, 0, m % per, 0, 0))
    s16 = jax.ShapeDtypeStruct((t, D_A), BF16)
    s1 = lambda c: jax.ShapeDtypeStruct((b, N_J // 8, per, 8 * IN_PROJ_RES, c), BF16)
    in_specs = [pl.BlockSpec((IN_PROJ_TM, D_MODEL), lambda m: (m, 0)),
                pl.BlockSpec((D_MODEL, IN_COLS), lambda m: (0, 0), pipeline_mode=pl.Buffered(1))]
    out_specs = [o16] * 5 + [o1(D_A)] * 5 + [o1(D_B), o1(8 * LANES)]
    out_shape = [s16] * 5 + [s1(D_A)] * 5 + [s1(D_B), s1(8 * LANES)]
    args = [h, w]
    if nxt is not None:
        wf, layer = nxt
        chunk = (wf.shape[1] // steps, wf.shape[2])
        in_specs.append(pl.BlockSpec((None,) + chunk, lambda m: (layer, m, 0)))
        out_specs.append(pl.BlockSpec(chunk, lambda m: (m, 0)))
        out_shape.append(jax.ShapeDtypeStruct(wf.shape[1:], BF16))
        args.append(wf)
    outs = pl.pallas_call(
        _in_proj_kernel,
        grid=(steps,),
        in_specs=in_specs,
        out_specs=out_specs,
        out_shape=out_shape,
        compiler_params=pltpu.CompilerParams(
            dimension_semantics=("arbitrary",), vmem_limit_bytes=VMEM_LIMIT),
        name="in_proj",
    )(*args)
    qkv, cast = outs[:N_IN_PROJ_OUT], outs[N_IN_PROJ_OUT:]
    return [o.reshape(b, SEQ, o.shape[-1]) for o in qkv], list(cast)


def _lane_lo():
    return lax.broadcasted_iota(jnp.int32, (BLK, LANES), 1) < HEAD_DIM


def _rows(ref, chunks):
    return jnp.concatenate([ref[pl.ds(start, size), :] for start, size in chunks], axis=0)


def _window(pair, window, src):
    chunks = [c for blk in window for c in src(blk)]
    return jnp.concatenate([_rows(ref, chunks) for ref in pair], axis=0)


def _scores_softmax(qs, ks, src, blocks, windows, bias0, bias1):
    nk = BLK * len(windows[0])
    q = jnp.stack([_rows(qs, src(blk)) for blk in blocks])
    kk = jnp.stack([_window(ks, w, src) for w in windows])
    s = jnp.einsum("gqd,gkd->gqk", q, kk, preferred_element_type=F32)
    l0 = s[:, :, :nk] + bias0
    l1 = s[:, :, nk:] + bias1
    m0 = jnp.max(l0, axis=2, keepdims=True)
    m1 = jnp.max(l1, axis=2, keepdims=True)
    p = jnp.concatenate([jnp.exp2(l0 - m0), jnp.exp2(l1 - m1)], axis=2).astype(BF16)
    return p, m0, m1


def _weighted_values(p, vs, ones_ref, src, windows):
    ones = ones_ref[...]
    vv = jnp.stack([jnp.concatenate([_window(vs, w, src), ones], axis=1) for w in windows])
    na = jnp.einsum("gqk,gkd->gqd", p, vv, preferred_element_type=F32)
    return na[:, :, :LANES], na[:, :, LANES:]


def _is_first(blk, per):
    return per is not None and blk % per == 0


def _band_windows(blocks, per):
    if per is None:
        return [(blk,) for blk in blocks]
    return [(blk, blk + 1) if _is_first(blk, per) else (blk - 1, blk) for blk in blocks]


def _band_bias(blocks, per, bias, bias_first):
    if not any(_is_first(blk, per) for blk in blocks):
        return bias[0], bias[1]
    return tuple(jnp.stack([(bias_first if _is_first(blk, per) else bias)[h] for blk in blocks])
                 for h in range(2))


def _pipelined(n_groups, first_half, second_half):
    first_half(0, 0)
    for g in range(n_groups - 1):
        first_half(g + 1, (g + 1) % 2)
        second_half(g, g % 2)
    second_half(n_groups - 1, (n_groups - 1) % 2)


def _first_block_bias(bias, bias_first):
    for h in range(2):
        bias_first[h, :, :BLK] = bias[h, :, BLK:]
        bias_first[h, :, BLK:] = jnp.full((BLK, BLK), -jnp.inf, F32)


def _dilated_branches(slopes_ref, hp, lo, q16, k16, v16, q1, k1, v1, dists, ones, biases,
                      num_s, m_s, den_s, p_scr):
    d16_ref, d4_ref, d1_ref = dists
    e1_ref, e2_ref = ones
    b16, b4, b4f, b1, b1f = biases
    ns0 = -LOG2E * slopes_ref[2 * hp]
    ns1 = -LOG2E * slopes_ref[2 * hp + 1]
    for bias, dist in ((b16, d16_ref), (b4, d4_ref), (b1, d1_ref)):
        bias[0] = ns0 * dist[...]
        bias[1] = ns1 * dist[...]
    _first_block_bias(b4, b4f)
    _first_block_bias(b1, b1f)

    def chunks16(blk):
        return [(blk * BLK, BLK)]

    def chunks4(blk):
        r4, mb = divmod(blk, 4)
        return [((r4 + 4 * t) * N_J + 32 * mb, 32) for t in range(4)]

    def chunks1_residue_major(blk):
        return [(r * N_J + 8 * blk, 8) for r in range(N_RES)]

    def scatter(dst_scr, branch, x, chunks):
        row = 0
        for start, size in chunks:
            dst_scr[branch, pl.ds(start, size), :] = x[row:row + size]
            row += size

    def blocks_of(g):
        return [g * GROUP + i for i in range(GROUP)]

    def branch_halves(qs, ks, vs, ones_ref, src, dst, bias, bias_first, per, branch):
        def first_half(g, slot):
            blocks = blocks_of(g)
            windows = _band_windows(blocks, per)
            bias0, bias1 = _band_bias(blocks, per, bias, bias_first)
            p, m0, m1 = _scores_softmax(qs, ks, src, blocks, windows, bias0, bias1)
            p_scr[slot, :, :, :p.shape[2]] = p
            for i, blk in enumerate(blocks):
                scatter(m_s, branch, jnp.where(lo, m0[i], m1[i]), dst(blk))

        def second_half(g, slot):
            blocks = blocks_of(g)
            windows = _band_windows(blocks, per)
            width = 2 * BLK * len(windows[0])
            num, den = _weighted_values(p_scr[slot, :, :, :width], vs, ones_ref, src, windows)
            for i, blk in enumerate(blocks):
                scatter(num_s, branch, num[i], dst(blk))
                scatter(den_s, branch, den[i], dst(blk))

        return first_half, second_half

    _pipelined(N_GROUPS, *branch_halves(q16, k16, v16, e1_ref, chunks16, chunks16,
                                        b16, None, None, 2))
    _pipelined(N_GROUPS, *branch_halves(q16, k16, v16, e2_ref, chunks4, chunks4,
                                        b4, b4f, 4, 1))
    _pipelined(N_GROUPS, *branch_halves(q1, k1, v1, e2_ref, chunks16, chunks1_residue_major,
                                        b1, b1f, N_BLK, 0))


def _sliding_window(slopes_ref, sinks_ref, hp, lo, q1, k1, v1, d1_ref, e2_ref, b1, b1f,
                    o_scr, p_scr, m_scr):
    sink = jnp.where(lo, LOG2E * sinks_ref[2 * hp], LOG2E * sinks_ref[2 * hp + 1])
    b1[0] = (-LOG2E * slopes_ref[2 * hp]) * d1_ref[...]
    b1[1] = (-LOG2E * slopes_ref[2 * hp + 1]) * d1_ref[...]
    _first_block_bias(b1, b1f)

    def src(blk):
        return [(blk * BLK, BLK)]

    def blocks_of(g):
        return [g * GROUP + i for i in range(GROUP)]

    def first_half(g, slot):
        blocks = blocks_of(g)
        bias0, bias1 = _band_bias(blocks, N_BLK, b1, b1f)
        p, m0, m1 = _scores_softmax(q1, k1, src, blocks, _band_windows(blocks, N_BLK),
                                    bias0, bias1)
        p_scr[slot] = p
        for i in range(GROUP):
            m_scr[slot, i] = jnp.where(lo, m0[i], m1[i])

    def second_half(g, slot):
        blocks = blocks_of(g)
        num, den = _weighted_values(p_scr[slot], v1, e2_ref, src, _band_windows(blocks, N_BLK))
        for i, blk in enumerate(blocks):
            m = m_scr[slot, i]
            m2 = jnp.maximum(m, sink)
            c = jnp.exp2(m - m2)
            out = (num[i] * c) / (den[i] * c + jnp.exp2(sink - m2))
            for r in range(N_RES):
                o_scr[pl.ds(r * N_J + 8 * blk, 8), :] = out[r * 8:(r + 1) * 8]

    _pipelined(N_GROUPS, first_half, second_half)


N_ATTN_IN, N_ATTN_OUT, N_ATTN_SCRATCH = 24, 2, 14


def _attn_kernel(*refs):
    n_cast = (len(refs) - N_ATTN_IN - N_ATTN_OUT - N_ATTN_SCRATCH) // 2
    (slopes_a_ref, slopes_b_ref, sinks_ref, q16, k16l, k16h, v16l, v16h, q1, k1l, k1h, v1l, v1h,
     qb, kbl, kbh, vbl, vbh, d16_ref, d4_ref, d1a_ref, d1b_ref, e1_ref, e2_ref) = refs[:N_ATTN_IN]
    cast_src = refs[N_ATTN_IN:N_ATTN_IN + n_cast]
    oa_ref, ob_ref = refs[N_ATTN_IN + n_cast:N_ATTN_IN + n_cast + N_ATTN_OUT]
    cast_dst = refs[N_ATTN_IN + n_cast + N_ATTN_OUT:len(refs) - N_ATTN_SCRATCH]
    (b16, b4, b4f, b1, b1f, bb, bbf, num_s, m_s, den_s, pa_scr, pb_scr, mb_scr,
     ob_scr) = refs[len(refs) - N_ATTN_SCRATCH:]
    hp = pl.program_id(1)
    lo = _lane_lo()
    _dilated_branches(slopes_a_ref, hp, lo, q16, (k16l, k16h), (v16l, v16h),
                      q1, (k1l, k1h), (v1l, v1h), (d16_ref, d4_ref, d1a_ref),
                      (e1_ref, e2_ref), (b16, b4, b4f, b1, b1f), num_s, m_s, den_s, pa_scr)
    _sliding_window(slopes_b_ref, sinks_ref, hp, lo, qb, (kbl, kbh), (vbl, vbh),
                    d1b_ref, e2_ref, bb, bbf, ob_scr, pb_scr, mb_scr)
    _cast_chunks(cast_src, cast_dst)

    def finish(g, carry):
        rows = pl.ds(pl.multiple_of(g * BLK, BLK), BLK)
        m = [m_s[i, rows, :] for i in range(3)]
        m_all = jnp.maximum(jnp.maximum(m[0], m[1]), m[2])
        w = [jnp.exp2(mi - m_all) for mi in m]
        numer = num_s[0, rows, :] * w[0] + num_s[1, rows, :] * w[1] + num_s[2, rows, :] * w[2]
        denom = den_s[0, rows, :] * w[0] + den_s[1, rows, :] * w[1] + den_s[2, rows, :] * w[2]
        oa_ref[rows, :] = (numer / denom).astype(oa_ref.dtype)
        ob_ref[rows, :] = ob_scr[rows, :].astype(ob_ref.dtype)
        return carry

    lax.fori_loop(0, N_BLK, finish, 0, unroll=2)


def _attention(qkv_a, qb1, kvb1, slopes_a, slopes_b, sinks, d16, d4, d1a, d1b, e1, e2, nxt):
    b = qb1.shape[0]
    n_pairs = N_HEADS_A // 2
    assert N_HEADS_B // 2 == n_pairs
    pairs_per_kv = n_pairs // N_KV_B
    smem = pl.BlockSpec(memory_space=pltpu.SMEM)
    col = pl.BlockSpec((None, SEQ, LANES), lambda i, hp: (i, 0, hp))
    kv_tile = lambda tile: pl.BlockSpec(
        (None, SEQ, LANES), lambda i, hp, tile=tile: (i, 0, tile + 2 * (hp // pairs_per_kv)))
    const = lambda shape: pl.BlockSpec(shape, lambda i, hp: (0, 0))
    band = pltpu.VMEM((2, BLK, 2 * BLK), F32)
    stat = pltpu.VMEM((3, SEQ, LANES), F32)
    probs = pltpu.VMEM((2, GROUP, BLK, 4 * BLK), BF16)
    in_specs = [smem, smem, smem] + [col] * 11 + [
        kv_tile(0), kv_tile(1), kv_tile(4), kv_tile(5),
        const((BLK, BLK)), const((BLK, 2 * BLK)), const((BLK, 2 * BLK)),
        const((BLK, 2 * BLK)), const((2 * BLK, LANES)), const((4 * BLK, LANES))]
    out_specs = [col, col]
    out_shape = [jax.ShapeDtypeStruct((b, SEQ, D_A), BF16),
                 jax.ShapeDtypeStruct((b, SEQ, D_B), BF16)]
    args = [slopes_a, slopes_b, sinks, *qkv_a, qb1, kvb1, kvb1, kvb1, kvb1,
            d16, d4, d1a, d1b, e1, e2]
    assert len(in_specs) == len(args) == N_ATTN_IN
    if nxt is not None:
        *stacked, layer = nxt
        for wf in stacked:
            chunk = (wf.shape[1] // (b * n_pairs), wf.shape[2])
            in_specs.append(pl.BlockSpec((None,) + chunk,
                                         lambda i, hp: (layer, i * n_pairs + hp, 0)))
            out_specs.append(pl.BlockSpec(chunk, lambda i, hp: (i * n_pairs + hp, 0)))
            out_shape.append(jax.ShapeDtypeStruct(wf.shape[1:], BF16))
        args += stacked
    return pl.pallas_call(
        _attn_kernel,
        grid=(b, n_pairs),
        in_specs=in_specs,
        out_specs=out_specs,
        out_shape=out_shape,
        scratch_shapes=[pltpu.VMEM((2, BLK, BLK), F32), band, band, band, band, band, band,
                        stat, stat, stat, probs, probs,
                        pltpu.VMEM((2, GROUP, BLK, LANES), F32),
                        pltpu.VMEM((SEQ, LANES), F32)],
        compiler_params=pltpu.CompilerParams(
            dimension_semantics=("arbitrary", "arbitrary"), vmem_limit_bytes=VMEM_LIMIT),
        name="attention",
    )(*args)


def _cast_chunks(src_refs, dst_refs):
    for src, dst in zip(src_refs, dst_refs):
        dst[...] = src[...].astype(BF16)


def _out_proj_kernel(a_ref, b_ref, x_ref, ga_ref, gb_ref, w_ref, gf_ref, *rest):
    n_cast = (len(rest) - 2) // 2
    cast_src, (xo_ref, h_ref), cast_dst = rest[:n_cast], rest[n_cast:n_cast + 2], rest[n_cast + 2:]
    na = _rms(a_ref[...].astype(F32), ga_ref[...]).astype(BF16)
    nb = _rms(b_ref[...].astype(F32), gb_ref[...]).astype(BF16)
    mix = jnp.concatenate([na, nb], axis=1)
    x = x_ref[...] + jnp.dot(mix, w_ref[...], preferred_element_type=F32)
    xo_ref[...] = x
    h_ref[...] = _rms(x, gf_ref[...]).astype(BF16)
    _cast_chunks(cast_src, cast_dst)


def _out_proj(a, b, x, ga, gb, w, gf, nxt):
    t = x.shape[0]
    tm = 512
    steps = t // tm
    row = lambda n: pl.BlockSpec((tm, n), lambda m: (m, 0))
    const = lambda r, c: pl.BlockSpec((r, c), lambda m: (0, 0))
    in_specs = [row(D_A), row(D_B), row(D_MODEL), const(1, D_A), const(1, D_B),
                pl.BlockSpec((D_MIX, D_MODEL), lambda m: (0, 0), pipeline_mode=pl.Buffered(1)),
                const(1, D_MODEL)]
    out_specs = [row(D_MODEL), row(D_MODEL)]
    out_shape = [jax.ShapeDtypeStruct((t, D_MODEL), F32), jax.ShapeDtypeStruct((t, D_MODEL), BF16)]
    args = [a, b, x, ga.reshape(1, D_A), gb.reshape(1, D_B), w, gf.reshape(1, D_MODEL)]
    if nxt is not None:
        *stacked, layer = nxt
        for wf in stacked:
            rows, cols = wf.shape[1] // steps, wf.shape[2]
            in_specs.append(pl.BlockSpec((None, rows, cols), lambda m: (layer, m, 0)))
            out_specs.append(pl.BlockSpec((rows, cols), lambda m: (m, 0)))
            out_shape.append(jax.ShapeDtypeStruct(wf.shape[1:], BF16))
        args += stacked
    return pl.pallas_call(
        _out_proj_kernel,
        grid=(steps,),
        in_specs=in_specs,
        out_specs=out_specs,
        out_shape=out_shape,
        compiler_params=pltpu.CompilerParams(
            dimension_semantics=("arbitrary",), vmem_limit_bytes=VMEM_LIMIT),
        name="out_proj",
    )(*args)


GATE_UP_TM, GATE_UP_TF = 2048, 512
DOWN_TM = 256


def _gate_up_kernel(h_ref, wg_ref, wu_ref, *rest):
    n_cast = (len(rest) - 1) // 2
    cast_src, act_ref, cast_dst = rest[:n_cast], rest[n_cast], rest[n_cast + 1:]
    h = h_ref[...]
    gate = jnp.dot(h, wg_ref[...], preferred_element_type=F32)
    up = jnp.dot(h, wu_ref[...], preferred_element_type=F32)
    act_ref[...] = (jax.nn.silu(gate) * up).astype(BF16)
    _cast_chunks(cast_src, cast_dst)


def _gate_up(h, wg, wu, nxt):
    t = h.shape[0]
    tm, tf = GATE_UP_TM, GATE_UP_TF
    n_m, n_f = t // tm, D_FF // tf
    w_spec = pl.BlockSpec((D_MODEL, tf), lambda m, f: (0, f))
    in_specs = [pl.BlockSpec((tm, D_MODEL), lambda m, f: (m, 0)), w_spec, w_spec]
    out_specs = [pl.BlockSpec((tm, tf), lambda m, f: (m, f))]
    out_shape = [jax.ShapeDtypeStruct((t, D_FF), BF16)]
    args = [h, wg, wu]
    if nxt is not None:
        *stacked, layer = nxt
        chunk = (D_MODEL // n_m, D_FF // n_f)
        for wf in stacked:
            in_specs.append(pl.BlockSpec((None,) + chunk, lambda m, f: (layer, m, f)))
            out_specs.append(pl.BlockSpec(chunk, lambda m, f: (m, f)))
            out_shape.append(jax.ShapeDtypeStruct(wf.shape[1:], BF16))
        args += stacked
    return pl.pallas_call(
        _gate_up_kernel,
        grid=(n_m, n_f),
        in_specs=in_specs,
        out_specs=out_specs,
        out_shape=out_shape,
        compiler_params=pltpu.CompilerParams(
            dimension_semantics=("arbitrary", "arbitrary"), vmem_limit_bytes=VMEM_LIMIT),
        name="ffn_gate_up",
    )(*args)


def _down_kernel(act_ref, wd_ref, x_ref, gn_ref, *rest):
    n_cast = (len(rest) - 2) // 2
    cast_src, (xo_ref, hn_ref), cast_dst = rest[:n_cast], rest[n_cast:n_cast + 2], rest[n_cast + 2:]
    x = x_ref[...] + jnp.dot(act_ref[...], wd_ref[...], preferred_element_type=F32)
    xo_ref[...] = x
    hn_ref[...] = _rms(x, gn_ref[...]).astype(hn_ref.dtype)
    _cast_chunks(cast_src, cast_dst)


def _down(act, wd, x, gn, normed_dtype, nxt):
    t = x.shape[0]
    tm = DOWN_TM
    steps = t // tm
    row = lambda n: pl.BlockSpec((tm, n), lambda m: (m, 0))
    in_specs = [row(D_FF),
                pl.BlockSpec((D_FF, D_MODEL), lambda m: (0, 0), pipeline_mode=pl.Buffered(1)),
                row(D_MODEL), pl.BlockSpec((1, D_MODEL), lambda m: (0, 0))]
    out_specs = [row(D_MODEL), row(D_MODEL)]
    out_shape = [jax.ShapeDtypeStruct((t, D_MODEL), F32),
                 jax.ShapeDtypeStruct((t, D_MODEL), normed_dtype)]
    args = [act, wd, x, gn.reshape(1, D_MODEL)]
    if nxt is not None:
        w_down, layer = nxt
        chunk = (D_FF // steps, D_MODEL)
        in_specs.append(pl.BlockSpec((None,) + chunk, lambda m: (layer, m, 0)))
        out_specs.append(pl.BlockSpec(chunk, lambda m: (m, 0)))
        out_shape.append(jax.ShapeDtypeStruct(w_down.shape[1:], BF16))
        args.append(w_down)
    return pl.pallas_call(
        _down_kernel,
        grid=(steps,),
        in_specs=in_specs,
        out_specs=out_specs,
        out_shape=out_shape,
        compiler_params=pltpu.CompilerParams(
            dimension_semantics=("arbitrary",), vmem_limit_bytes=VMEM_LIMIT),
        name="ffn_down",
    )(*args)


def kernel(x, attn_norm, w_in, sinks, out_norm_a, out_norm_b, w_out,
           ffn_norm, w_gate, w_up, w_down, final_norm):
    b, s_len, d = x.shape
    assert (s_len, d) == (SEQ, D_MODEL)
    depth = w_in.shape[0]

    idx = jnp.arange(N_ALIBI, dtype=F32)
    slopes = jnp.exp2(-8.0 * (idx + 1.0) / N_ALIBI)
    slopes_a, slopes_b = slopes[0::2], slopes[1::2]
    d16, d4, d1a, d1b = (jnp.asarray(p) for p in _dist_patterns())
    e1 = jnp.asarray(_ones_pattern(BLK), BF16)
    e2 = jnp.asarray(_ones_pattern(2 * BLK), BF16)

    wi = w_in[0].astype(BF16)
    xp, h = _permute_norm(x, attn_norm[0])
    for l in range(depth):
        first, last = l == 0, l + 1 == depth
        (*qkv_a, qb1, kvb1), cast = _in_proj(h, wi, b, (w_out, 0) if first else None)
        if first:
            (wo,) = cast
        out_a, out_b, *ffn_w = _attention(qkv_a, qb1, kvb1, slopes_a, slopes_b, sinks[l],
                                          d16, d4, d1a, d1b, e1, e2,
                                          (w_gate, w_up, w_down, 0) if first else None)
        if first:
            wg, wu, wd = ffn_w
        out_a, out_b = out_a.reshape(b * SEQ, D_A), out_b.reshape(b * SEQ, D_B)
        xp, h, *nxt_proj = _out_proj(out_a, out_b, xp, out_norm_a[l], out_norm_b[l], wo,
                                     ffn_norm[l], None if last else (w_in, w_out, l + 1))
        act, *nxt_gu = _gate_up(h, wg, wu, None if last else (w_gate, w_up, l + 1))
        xp, h, *nxt_d = _down(act, wd, xp, final_norm if last else attn_norm[l + 1],
                              F32 if last else BF16, None if last else (w_down, l + 1))
        if not last:
            (wi, wo), (wg, wu), (wd,) = nxt_proj, nxt_gu, nxt_d
    return _unpermute(h, b)
```

```python
import math

import numpy as np
import jax
import jax.numpy as jnp
from jax import lax
from jax.experimental import pallas as pl
from jax.experimental.pallas import tpu as pltpu

F32 = jnp.float32
BF16 = jnp.bfloat16

D_MODEL = 2048
SEQ = 2048
HEAD_DIM = 64
N_HEADS_A = 16
N_HEADS_B = 16
N_KV_B = 2
D_A = N_HEADS_A * HEAD_DIM
D_B = N_HEADS_B * HEAD_DIM
D_KV_B = N_KV_B * HEAD_DIM
D_MIX = D_A + D_B
IN_COLS = 3 * D_A + D_B + 2 * D_KV_B
N_ALIBI = N_HEADS_A + N_HEADS_B
D_FF = 5632
EPS = 1e-6
SWA_WINDOW = 128
DIL_MAX_DIST = 128

LANES = 128
BLK = 128
N_RES = 16
N_J = SEQ // N_RES
N_BLK = SEQ // BLK
GROUP = 4
N_GROUPS = N_BLK // GROUP
LOG2E = math.log2(math.e)
VMEM_LIMIT = 56 * 1024 * 1024


def _rms(x, g):
    return x * lax.rsqrt(jnp.mean(x * x, axis=-1, keepdims=True) + EPS) * g


def _dist_patterns():
    inf = np.float32(np.inf)
    j = np.arange(BLK)
    d16 = (j[:, None] - j[None, :]).astype(np.float32)
    d16 = np.where(d16 >= 0, 16.0 * d16, inf).astype(np.float32)

    t = np.arange(BLK) // 32
    jj = np.arange(BLK) % 32
    sub = 4 * jj + t
    cur = sub[:, None] - sub[None, :]
    prev = cur + BLK
    d4 = np.concatenate([prev, cur], axis=1).astype(np.float32)
    d4 = np.where((d4 >= 0) & (d4 <= DIL_MAX_DIST), 4.0 * d4, inf).astype(np.float32)

    r = np.arange(BLK) // 8
    jj = np.arange(BLK) % 8
    pos = 16 * jj + r
    cur = pos[:, None] - pos[None, :]
    prev = cur + BLK
    d1 = np.concatenate([prev, cur], axis=1).astype(np.float32)
    d1a = np.where((d1 >= 0) & (d1 <= DIL_MAX_DIST), d1, inf).astype(np.float32)
    d1b = np.where((d1 >= 0) & (d1 <= SWA_WINDOW - 1), d1, inf).astype(np.float32)
    return d16, d4, d1a, d1b


def _ones_pattern(nk):
    e = np.zeros((2 * nk, LANES), np.float32)
    e[:nk, :HEAD_DIM] = 1.0
    e[nk:, HEAD_DIM:] = 1.0
    return e


PERM_J, PERM_R = 64, 8


def _permute_norm_kernel(x_ref, g_ref, xp_ref, h_ref):
    x = jnp.swapaxes(x_ref[...], 0, 1)
    xp_ref[...] = x
    h_ref[...] = _rms(x, g_ref[...]).astype(BF16)


def _permute_norm(x, g):
    b = x.shape[0]
    nat = pl.BlockSpec((None, PERM_J, PERM_R, D_MODEL), lambda i, r, j: (i, j, r, 0))
    res = pl.BlockSpec((None, PERM_R, PERM_J, D_MODEL), lambda i, r, j: (i, r, j, 0))
    xp, h = pl.pallas_call(
        _permute_norm_kernel,
        grid=(b, N_RES // PERM_R, N_J // PERM_J),
        in_specs=[nat, pl.BlockSpec((1, 1, D_MODEL), lambda i, r, j: (0, 0, 0))],
        out_specs=[res, res],
        out_shape=[jax.ShapeDtypeStruct((b, N_RES, N_J, D_MODEL), F32),
                   jax.ShapeDtypeStruct((b, N_RES, N_J, D_MODEL), BF16)],
        compiler_params=pltpu.CompilerParams(
            dimension_semantics=("arbitrary",) * 3, vmem_limit_bytes=VMEM_LIMIT),
        name="permute_norm",
    )(x.reshape(b, N_J, N_RES, D_MODEL), g.reshape(1, 1, D_MODEL))
    return xp.reshape(b * SEQ, D_MODEL), h.reshape(b * SEQ, D_MODEL)


def _unpermute_kernel(y_ref, o_ref):
    o_ref[...] = jnp.swapaxes(y_ref[...], 0, 1)


def _unpermute(y, b):
    nat = pl.BlockSpec((None, PERM_J, PERM_R, D_MODEL), lambda i, r, j: (i, j, r, 0))
    res = pl.BlockSpec((None, PERM_R, PERM_J, D_MODEL), lambda i, r, j: (i, r, j, 0))
    out = pl.pallas_call(
        _unpermute_kernel,
        grid=(b, N_RES // PERM_R, N_J // PERM_J),
        in_specs=[res],
        out_specs=nat,
        out_shape=jax.ShapeDtypeStruct((b, N_J, N_RES, D_MODEL), F32),
        compiler_params=pltpu.CompilerParams(
            dimension_semantics=("arbitrary",) * 3, vmem_limit_bytes=VMEM_LIMIT),
        name="unpermute",
    )(y.reshape(b, N_RES, N_J, D_MODEL))
    return out.reshape(b, SEQ, D_MODEL)


IN_PROJ_RES = 4
IN_PROJ_TM = IN_PROJ_RES * N_J
IN_PROJ_STEPS = N_RES // IN_PROJ_RES
Q_SCALE = LOG2E * HEAD_DIM ** -0.5


N_IN_PROJ_OUT = 12


def _in_proj_kernel(h_ref, w_ref, *rest):
    n_cast = (len(rest) - N_IN_PROJ_OUT) // 2
    cast_src, cast_dst = rest[:n_cast], rest[n_cast + N_IN_PROJ_OUT:]
    q16, k16l, k16h, v16l, v16h, q1, k1l, k1h, v1l, v1h, qb1, kvb1 = rest[n_cast:n_cast + N_IN_PROJ_OUT]
    h = h_ref[...]

    def order1(x):
        n = x.shape[1]
        pieces = [x[r * N_J + jb * 8:r * N_J + jb * 8 + 8]
                  for jb in range(N_J // 8) for r in range(IN_PROJ_RES)]
        return jnp.concatenate(pieces, axis=0).reshape(N_J // 8, 8 * IN_PROJ_RES, n)

    def proj(col, width, scale=None):
        x = jnp.dot(h, w_ref[:, col:col + width], preferred_element_type=F32)
        return x if scale is None else x * scale

    def store2(x, o16, o1):
        o16[...] = x.astype(BF16)
        o1[...] = order1(x).astype(BF16)

    def first_head(width):
        return (lax.broadcasted_iota(jnp.int32, (IN_PROJ_TM, width), 1) & HEAD_DIM) == 0

    store2(proj(0, D_A, Q_SCALE), q16, q1)
    lo = first_head(D_A)
    for i, outs in ((1, ((k16l, k1l), (k16h, k1h))), (2, ((v16l, v1l), (v16h, v1h)))):
        x = proj(i * D_A, D_A)
        store2(jnp.where(lo, x, 0.0), *outs[0])
        store2(jnp.where(lo, 0.0, x), *outs[1])
    qb1[...] = order1(proj(3 * D_A, D_B, Q_SCALE)).astype(BF16)

    x = proj(3 * D_A + D_B, 2 * D_KV_B)
    lo = first_head(LANES)
    tiles = []
    for part in (x[:, :LANES], x[:, LANES:]):
        swapped = pltpu.roll(part, HEAD_DIM, 1)
        tiles += [jnp.where(lo, part, 0.0), jnp.where(lo, 0.0, swapped),
                  jnp.where(lo, swapped, 0.0), jnp.where(lo, 0.0, part)]
    kvb1[...] = order1(jnp.concatenate(tiles, axis=1)).astype(BF16)
    _cast_chunks(cast_src, cast_dst)


def _in_proj(h, w, b, nxt):
    t = h.shape[0]
    steps = t // IN_PROJ_TM
    per = IN_PROJ_STEPS
    o16 = pl.BlockSpec((IN_PROJ_TM, D_A), lambda m: (m, 0))
    o1 = lambda c: pl.BlockSpec((None, N_J // 8, None, 8 * IN_PROJ_RES, c),
                                lambda m: (m // per, 0, m % per, 0, 0))
    s16 = jax.ShapeDtypeStruct((t, D_A), BF16)
    s1 = lambda c: jax.ShapeDtypeStruct((b, N_J // 8, per, 8 * IN_PROJ_RES, c), BF16)
    in_specs = [pl.BlockSpec((IN_PROJ_TM, D_MODEL), lambda m: (m, 0)),
                pl.BlockSpec((D_MODEL, IN_COLS), lambda m: (0, 0), pipeline_mode=pl.Buffered(1))]
    out_specs = [o16] * 5 + [o1(D_A)] * 5 + [o1(D_B), o1(8 * LANES)]
    out_shape = [s16] * 5 + [s1(D_A)] * 5 + [s1(D_B), s1(8 * LANES)]
    args = [h, w]
    if nxt is not None:
        wf, layer = nxt
        chunk = (wf.shape[1] // steps, wf.shape[2])
        in_specs.append(pl.BlockSpec((None,) + chunk, lambda m: (layer, m, 0)))
        out_specs.append(pl.BlockSpec(chunk, lambda m: (m, 0)))
        out_shape.append(jax.ShapeDtypeStruct(wf.shape[1:], BF16))
        args.append(wf)
    outs = pl.pallas_call(
        _in_proj_kernel,
        grid=(steps,),
        in_specs=in_specs,
        out_specs=out_specs,
        out_shape=out_shape,
        compiler_params=pltpu.CompilerParams(
            dimension_semantics=("arbitrary",), vmem_limit_bytes=VMEM_LIMIT),
        name="in_proj",
    )(*args)
    qkv, cast = outs[:N_IN_PROJ_OUT], outs[N_IN_PROJ_OUT:]
    return [o.reshape(b, SEQ, o.shape[-1]) for o in qkv], list(cast)


def _lane_lo():
    return lax.broadcasted_iota(jnp.int32, (BLK, LANES), 1) < HEAD_DIM


def _rows(ref, chunks):
    return jnp.concatenate([ref[pl.ds(start, size), :] for start, size in chunks], axis=0)


def _window(pair, window, src):
    chunks = [c for blk in window for c in src(blk)]
    return jnp.concatenate([_rows(ref, chunks) for ref in pair], axis=0)


def _scores_softmax(qs, ks, src, blocks, windows, bias0, bias1):
    nk = BLK * len(windows[0])
    q = jnp.stack([_rows(qs, src(blk)) for blk in blocks])
    kk = jnp.stack([_window(ks, w, src) for w in windows])
    s = jnp.einsum("gqd,gkd->gqk", q, kk, preferred_element_type=F32)
    l0 = s[:, :, :nk] + bias0
    l1 = s[:, :, nk:] + bias1
    m0 = jnp.max(l0, axis=2, keepdims=True)
    m1 = jnp.max(l1, axis=2, keepdims=True)
    p = jnp.concatenate([jnp.exp2(l0 - m0), jnp.exp2(l1 - m1)], axis=2).astype(BF16)
    return p, m0, m1


def _weighted_values(p, vs, ones_ref, src, windows):
    ones = ones_ref[...]
    vv = jnp.stack([jnp.concatenate([_window(vs, w, src), ones], axis=1) for w in windows])
    na = jnp.einsum("gqk,gkd->gqd", p, vv, preferred_element_type=F32)
    return na[:, :, :LANES], na[:, :, LANES:]


def _is_first(blk, per):
    return per is not None and blk % per == 0


def _band_windows(blocks, per):
    if per is None:
        return [(blk,) for blk in blocks]
    return [(blk, blk + 1) if _is_first(blk, per) else (blk - 1, blk) for blk in blocks]


def _band_bias(blocks, per, bias, bias_first):
    if not any(_is_first(blk, per) for blk in blocks):
        return bias[0], bias[1]
    return tuple(jnp.stack([(bias_first if _is_first(blk, per) else bias)[h] for blk in blocks])
                 for h in range(2))


def _pipelined(n_groups, first_half, second_half):
    first_half(0, 0)
    for g in range(n_groups - 1):
        first_half(g + 1, (g + 1) % 2)
        second_half(g, g % 2)
    second_half(n_groups - 1, (n_groups - 1) % 2)


def _first_block_bias(bias, bias_first):
    for h in range(2):
        bias_first[h, :, :BLK] = bias[h, :, BLK:]
        bias_first[h, :, BLK:] = jnp.full((BLK, BLK), -jnp.inf, F32)


def _dilated_branches(slopes_ref, hp, lo, q16, k16, v16, q1, k1, v1, dists, ones, biases,
                      num_s, m_s, den_s, p_scr):
    d16_ref, d4_ref, d1_ref = dists
    e1_ref, e2_ref = ones
    b16, b4, b4f, b1, b1f = biases
    ns0 = -LOG2E * slopes_ref[2 * hp]
    ns1 = -LOG2E * slopes_ref[2 * hp + 1]
    for bias, dist in ((b16, d16_ref), (b4, d4_ref), (b1, d1_ref)):
        bias[0] = ns0 * dist[...]
        bias[1] = ns1 * dist[...]
    _first_block_bias(b4, b4f)
    _first_block_bias(b1, b1f)

    def chunks16(blk):
        return [(blk * BLK, BLK)]

    def chunks4(blk):
        r4, mb = divmod(blk, 4)
        return [((r4 + 4 * t) * N_J + 32 * mb, 32) for t in range(4)]

    def chunks1_residue_major(blk):
        return [(r * N_J + 8 * blk, 8) for r in range(N_RES)]

    def scatter(dst_scr, branch, x, chunks):
        row = 0
        for start, size in chunks:
            dst_scr[branch, pl.ds(start, size), :] = x[row:row + size]
            row += size

    def blocks_of(g):
        return [g * GROUP + i for i in range(GROUP)]

    def gather(src_scr, slot, chunks):
        return jnp.concatenate([src_scr[slot, pl.ds(start, size), :] for start, size in chunks],
                               axis=0)

    def branch_halves(qs, ks, vs, ones_ref, src, dst, bias, bias_first, per, fold):
        def first_half(g, slot):
            blocks = blocks_of(g)
            windows = _band_windows(blocks, per)
            bias0, bias1 = _band_bias(blocks, per, bias, bias_first)
            p, m0, m1 = _scores_softmax(qs, ks, src, blocks, windows, bias0, bias1)
            p_scr[slot, :, :, :p.shape[2]] = p
            for i, blk in enumerate(blocks):
                scatter(m_s, 1 if fold else 0, jnp.where(lo, m0[i], m1[i]), dst(blk))

        def second_half(g, slot):
            blocks = blocks_of(g)
            windows = _band_windows(blocks, per)
            width = 2 * BLK * len(windows[0])
            num, den = _weighted_values(p_scr[slot, :, :, :width], vs, ones_ref, src, windows)
            for i, blk in enumerate(blocks):
                rows = dst(blk)
                n, d = num[i], den[i]
                if fold:
                    m_new, m_old = gather(m_s, 1, rows), gather(m_s, 0, rows)
                    m_all = jnp.maximum(m_new, m_old)
                    w_new, w_old = jnp.exp2(m_new - m_all), jnp.exp2(m_old - m_all)
                    n = n * w_new + gather(num_s, 0, rows) * w_old
                    d = d * w_new + gather(den_s, 0, rows) * w_old
                    scatter(m_s, 0, m_all, rows)
                scatter(num_s, 0, n, rows)
                scatter(den_s, 0, d, rows)

        return first_half, second_half

    _pipelined(N_GROUPS, *branch_halves(q16, k16, v16, e1_ref, chunks16, chunks16,
                                        b16, None, None, False))
    _pipelined(N_GROUPS, *branch_halves(q16, k16, v16, e2_ref, chunks4, chunks4,
                                        b4, b4f, 4, True))
    _pipelined(N_GROUPS, *branch_halves(q1, k1, v1, e2_ref, chunks16, chunks1_residue_major,
                                        b1, b1f, N_BLK, True))


def _sliding_window(slopes_ref, sinks_ref, hp, lo, q1, k1, v1, d1_ref, e2_ref, b1, b1f,
                    o_scr, p_scr, m_scr):
    sink = jnp.where(lo, LOG2E * sinks_ref[2 * hp], LOG2E * sinks_ref[2 * hp + 1])
    b1[0] = (-LOG2E * slopes_ref[2 * hp]) * d1_ref[...]
    b1[1] = (-LOG2E * slopes_ref[2 * hp + 1]) * d1_ref[...]
    _first_block_bias(b1, b1f)

    def src(blk):
        return [(blk * BLK, BLK)]

    def blocks_of(g):
        return [g * GROUP + i for i in range(GROUP)]

    def first_half(g, slot):
        blocks = blocks_of(g)
        bias0, bias1 = _band_bias(blocks, N_BLK, b1, b1f)
        p, m0, m1 = _scores_softmax(q1, k1, src, blocks, _band_windows(blocks, N_BLK),
                                    bias0, bias1)
        p_scr[slot] = p
        for i in range(GROUP):
            m_scr[slot, i] = jnp.where(lo, m0[i], m1[i])

    def second_half(g, slot):
        blocks = blocks_of(g)
        num, den = _weighted_values(p_scr[slot], v1, e2_ref, src, _band_windows(blocks, N_BLK))
        for i, blk in enumerate(blocks):
            m = m_scr[slot, i]
            m2 = jnp.maximum(m, sink)
            c = jnp.exp2(m - m2)
            out = (num[i] * c) / (den[i] * c + jnp.exp2(sink - m2))
            for r in range(N_RES):
                o_scr[pl.ds(r * N_J + 8 * blk, 8), :] = out[r * 8:(r + 1) * 8]

    _pipelined(N_GROUPS, first_half, second_half)


N_ATTN_IN, N_ATTN_OUT, N_ATTN_SCRATCH = 24, 2, 14


def _attn_kernel(*refs):
    n_cast = (len(refs) - N_ATTN_IN - N_ATTN_OUT - N_ATTN_SCRATCH) // 2
    (slopes_a_ref, slopes_b_ref, sinks_ref, q16, k16l, k16h, v16l, v16h, q1, k1l, k1h, v1l, v1h,
     qb, kbl, kbh, vbl, vbh, d16_ref, d4_ref, d1a_ref, d1b_ref, e1_ref, e2_ref) = refs[:N_ATTN_IN]
    cast_src = refs[N_ATTN_IN:N_ATTN_IN + n_cast]
    oa_ref, ob_ref = refs[N_ATTN_IN + n_cast:N_ATTN_IN + n_cast + N_ATTN_OUT]
    cast_dst = refs[N_ATTN_IN + n_cast + N_ATTN_OUT:len(refs) - N_ATTN_SCRATCH]
    (b16, b4, b4f, b1, b1f, bb, bbf, num_s, m_s, den_s, pa_scr, pb_scr, mb_scr,
     ob_scr) = refs[len(refs) - N_ATTN_SCRATCH:]
    hp = pl.program_id(1)
    lo = _lane_lo()
    _dilated_branches(slopes_a_ref, hp, lo, q16, (k16l, k16h), (v16l, v16h),
                      q1, (k1l, k1h), (v1l, v1h), (d16_ref, d4_ref, d1a_ref),
                      (e1_ref, e2_ref), (b16, b4, b4f, b1, b1f), num_s, m_s, den_s, pa_scr)
    _sliding_window(slopes_b_ref, sinks_ref, hp, lo, qb, (kbl, kbh), (vbl, vbh),
                    d1b_ref, e2_ref, bb, bbf, ob_scr, pb_scr, mb_scr)
    _cast_chunks(cast_src, cast_dst)

    def finish(g, carry):
        rows = pl.ds(pl.multiple_of(g * BLK, BLK), BLK)
        oa_ref[rows, :] = (num_s[0, rows, :] / den_s[0, rows, :]).astype(oa_ref.dtype)
        ob_ref[rows, :] = ob_scr[rows, :].astype(ob_ref.dtype)
        return carry

    lax.fori_loop(0, N_BLK, finish, 0, unroll=2)


def _attention(qkv_a, qb1, kvb1, slopes_a, slopes_b, sinks, d16, d4, d1a, d1b, e1, e2, nxt):
    b = qb1.shape[0]
    n_pairs = N_HEADS_A // 2
    assert N_HEADS_B // 2 == n_pairs
    pairs_per_kv = n_pairs // N_KV_B
    smem = pl.BlockSpec(memory_space=pltpu.SMEM)
    col = pl.BlockSpec((None, SEQ, LANES), lambda i, hp: (i, 0, hp))
    kv_tile = lambda tile: pl.BlockSpec(
        (None, SEQ, LANES), lambda i, hp, tile=tile: (i, 0, tile + 2 * (hp // pairs_per_kv)))
    const = lambda shape: pl.BlockSpec(shape, lambda i, hp: (0, 0))
    band = pltpu.VMEM((2, BLK, 2 * BLK), F32)
    stat = pltpu.VMEM((2, SEQ, LANES), F32)
    probs = pltpu.VMEM((2, GROUP, BLK, 4 * BLK), BF16)
    in_specs = [smem, smem, smem] + [col] * 11 + [
        kv_tile(0), kv_tile(1), kv_tile(4), kv_tile(5),
        const((BLK, BLK)), const((BLK, 2 * BLK)), const((BLK, 2 * BLK)),
        const((BLK, 2 * BLK)), const((2 * BLK, LANES)), const((4 * BLK, LANES))]
    out_specs = [col, col]
    out_shape = [jax.ShapeDtypeStruct((b, SEQ, D_A), BF16),
                 jax.ShapeDtypeStruct((b, SEQ, D_B), BF16)]
    args = [slopes_a, slopes_b, sinks, *qkv_a, qb1, kvb1, kvb1, kvb1, kvb1,
            d16, d4, d1a, d1b, e1, e2]
    assert len(in_specs) == len(args) == N_ATTN_IN
    if nxt is not None:
        *stacked, layer = nxt
        for wf in stacked:
            chunk = (wf.shape[1] // (b * n_pairs), wf.shape[2])
            in_specs.append(pl.BlockSpec((None,) + chunk,
                                         lambda i, hp: (layer, i * n_pairs + hp, 0)))
            out_specs.append(pl.BlockSpec(chunk, lambda i, hp: (i * n_pairs + hp, 0)))
            out_shape.append(jax.ShapeDtypeStruct(wf.shape[1:], BF16))
        args += stacked
    return pl.pallas_call(
        _attn_kernel,
        grid=(b, n_pairs),
        in_specs=in_specs,
        out_specs=out_specs,
        out_shape=out_shape,
        scratch_shapes=[pltpu.VMEM((2, BLK, BLK), F32), band, band, band, band, band, band,
                        stat, stat, stat, probs, probs,
                        pltpu.VMEM((2, GROUP, BLK, LANES), F32),
                        pltpu.VMEM((SEQ, LANES), F32)],
        compiler_params=pltpu.CompilerParams(
            dimension_semantics=("arbitrary", "arbitrary"), vmem_limit_bytes=VMEM_LIMIT),
        name="attention",
    )(*args)


def _cast_chunks(src_refs, dst_refs):
    for src, dst in zip(src_refs, dst_refs):
        dst[...] = src[...].astype(BF16)


def _out_proj_kernel(a_ref, b_ref, x_ref, ga_ref, gb_ref, w_ref, gf_ref, xo_ref, h_ref):
    na = _rms(a_ref[...].astype(F32), ga_ref[...]).astype(BF16)
    nb = _rms(b_ref[...].astype(F32), gb_ref[...]).astype(BF16)
    mix = jnp.concatenate([na, nb], axis=1)
    x = x_ref[...] + jnp.dot(mix, w_ref[...], preferred_element_type=F32)
    xo_ref[...] = x
    h_ref[...] = _rms(x, gf_ref[...]).astype(BF16)


def _out_proj(a, b, x, ga, gb, w, gf):
    t = x.shape[0]
    tm = 512
    row = lambda n: pl.BlockSpec((tm, n), lambda m: (m, 0))
    const = lambda r, c: pl.BlockSpec((r, c), lambda m: (0, 0))
    return pl.pallas_call(
        _out_proj_kernel,
        grid=(t // tm,),
        in_specs=[row(D_A), row(D_B), row(D_MODEL), const(1, D_A), const(1, D_B),
                  pl.BlockSpec((D_MIX, D_MODEL), lambda m: (0, 0), pipeline_mode=pl.Buffered(1)),
                  const(1, D_MODEL)],
        out_specs=[row(D_MODEL), row(D_MODEL)],
        out_shape=[jax.ShapeDtypeStruct((t, D_MODEL), F32),
                   jax.ShapeDtypeStruct((t, D_MODEL), BF16)],
        compiler_params=pltpu.CompilerParams(
            dimension_semantics=("arbitrary",), vmem_limit_bytes=VMEM_LIMIT),
        name="out_proj",
    )(a, b, x, ga.reshape(1, D_A), gb.reshape(1, D_B), w, gf.reshape(1, D_MODEL))


GATE_UP_TM, GATE_UP_TF = 2048, 512
DOWN_TM = 256


def _gate_up_kernel(h_ref, wg_ref, wu_ref, *rest):
    n_cast = (len(rest) - 1) // 2
    cast_src, act_ref, cast_dst = rest[:n_cast], rest[n_cast], rest[n_cast + 1:]
    h = h_ref[...]
    gate = jnp.dot(h, wg_ref[...], preferred_element_type=F32)
    up = jnp.dot(h, wu_ref[...], preferred_element_type=F32)
    act_ref[...] = (jax.nn.silu(gate) * up).astype(BF16)
    _cast_chunks(cast_src, cast_dst)


def _gate_up(h, wg, wu, nxt):
    t = h.shape[0]
    tm, tf = GATE_UP_TM, GATE_UP_TF
    n_m, n_f = t // tm, D_FF // tf
    w_spec = pl.BlockSpec((D_MODEL, tf), lambda m, f: (0, f))
    in_specs = [pl.BlockSpec((tm, D_MODEL), lambda m, f: (m, 0)), w_spec, w_spec]
    out_specs = [pl.BlockSpec((tm, tf), lambda m, f: (m, f))]
    out_shape = [jax.ShapeDtypeStruct((t, D_FF), BF16)]
    args = [h, wg, wu]
    if nxt is not None:
        *stacked, layer = nxt
        chunk = (D_MODEL // n_m, D_FF // n_f)
        for wf in stacked:
            in_specs.append(pl.BlockSpec((None,) + chunk, lambda m, f: (layer, m, f)))
            out_specs.append(pl.BlockSpec(chunk, lambda m, f: (m, f)))
            out_shape.append(jax.ShapeDtypeStruct(wf.shape[1:], BF16))
        args += stacked
    return pl.pallas_call(
        _gate_up_kernel,
        grid=(n_m, n_f),
        in_specs=in_specs,
        out_specs=out_specs,
        out_shape=out_shape,
        compiler_params=pltpu.CompilerParams(
            dimension_semantics=("arbitrary", "arbitrary"), vmem_limit_bytes=VMEM_LIMIT),
        name="ffn_gate_up",
    )(*args)


def _down_kernel(act_ref, wd_ref, x_ref, gn_ref, *rest):
    n_cast = (len(rest) - 2) // 2
    cast_src, (xo_ref, hn_ref), cast_dst = rest[:n_cast], rest[n_cast:n_cast + 2], rest[n_cast + 2:]
    x = x_ref[...] + jnp.dot(act_ref[...], wd_ref[...], preferred_element_type=F32)
    xo_ref[...] = x
    hn_ref[...] = _rms(x, gn_ref[...]).astype(hn_ref.dtype)
    _cast_chunks(cast_src, cast_dst)


def _down(act, wd, x, gn, normed_dtype, nxt):
    t = x.shape[0]
    tm = DOWN_TM
    steps = t // tm
    row = lambda n: pl.BlockSpec((tm, n), lambda m: (m, 0))
    in_specs = [row(D_FF),
                pl.BlockSpec((D_FF, D_MODEL), lambda m: (0, 0), pipeline_mode=pl.Buffered(1)),
                row(D_MODEL), pl.BlockSpec((1, D_MODEL), lambda m: (0, 0))]
    out_specs = [row(D_MODEL), row(D_MODEL)]
    out_shape = [jax.ShapeDtypeStruct((t, D_MODEL), F32),
                 jax.ShapeDtypeStruct((t, D_MODEL), normed_dtype)]
    args = [act, wd, x, gn.reshape(1, D_MODEL)]
    if nxt is not None:
        *stacked, layer = nxt
        for wf in stacked:
            chunk = (wf.shape[1] // steps, wf.shape[2])
            in_specs.append(pl.BlockSpec((None,) + chunk, lambda m: (layer, m, 0)))
            out_specs.append(pl.BlockSpec(chunk, lambda m: (m, 0)))
            out_shape.append(jax.ShapeDtypeStruct(wf.shape[1:], BF16))
        args += stacked
    return pl.pallas_call(
        _down_kernel,
        grid=(steps,),
        in_specs=in_specs,
        out_specs=out_specs,
        out_shape=out_shape,
        compiler_params=pltpu.CompilerParams(
            dimension_semantics=("arbitrary",), vmem_limit_bytes=VMEM_LIMIT),
        name="ffn_down",
    )(*args)


def kernel(x, attn_norm, w_in, sinks, out_norm_a, out_norm_b, w_out,
           ffn_norm, w_gate, w_up, w_down, final_norm):
    b, s_len, d = x.shape
    assert (s_len, d) == (SEQ, D_MODEL)
    depth = w_in.shape[0]

    idx = jnp.arange(N_ALIBI, dtype=F32)
    slopes = jnp.exp2(-8.0 * (idx + 1.0) / N_ALIBI)
    slopes_a, slopes_b = slopes[0::2], slopes[1::2]
    d16, d4, d1a, d1b = (jnp.asarray(p) for p in _dist_patterns())
    e1 = jnp.asarray(_ones_pattern(BLK), BF16)
    e2 = jnp.asarray(_ones_pattern(2 * BLK), BF16)

    wi = w_in[0].astype(BF16)
    xp, h = _permute_norm(x, attn_norm[0])
    for l in range(depth):
        first, last = l == 0, l + 1 == depth
        (*qkv_a, qb1, kvb1), cast = _in_proj(h, wi, b, (w_out, 0) if first else None)
        if first:
            (wo,) = cast
        out_a, out_b, *ffn_w = _attention(qkv_a, qb1, kvb1, slopes_a, slopes_b, sinks[l],
                                          d16, d4, d1a, d1b, e1, e2,
                                          (w_gate, w_up, w_down, 0) if first else None)
        if first:
            wg, wu, wd = ffn_w
        out_a, out_b = out_a.reshape(b * SEQ, D_A), out_b.reshape(b * SEQ, D_B)
        xp, h = _out_proj(out_a, out_b, xp, out_norm_a[l], out_norm_b[l], wo, ffn_norm[l])
        act, *nxt_gu = _gate_up(h, wg, wu, None if last else (w_gate, w_up, l + 1))
        xp, h, *nxt_d = _down(act, wd, xp, final_norm if last else attn_norm[l + 1],
                              F32 if last else BF16,
                              None if last else (w_down, w_in, w_out, l + 1))
        if not last:
            (wg, wu), (wd, wi, wo) = nxt_gu, nxt_d
    return _unpermute(h, b)
```

```python
import math

import numpy as np
import jax
import jax.numpy as jnp
from jax import lax
from jax.experimental import pallas as pl
from jax.experimental.pallas import tpu as pltpu

F32 = jnp.float32
BF16 = jnp.bfloat16

D_MODEL = 2048
SEQ = 2048
HEAD_DIM = 64
N_HEADS_A = 16
N_HEADS_B = 16
N_KV_B = 2
D_A = N_HEADS_A * HEAD_DIM
D_B = N_HEADS_B * HEAD_DIM
D_KV_B = N_KV_B * HEAD_DIM
D_MIX = D_A + D_B
IN_COLS = 3 * D_A + D_B + 2 * D_KV_B
N_ALIBI = N_HEADS_A + N_HEADS_B
D_FF = 5632
EPS = 1e-6
SWA_WINDOW = 128
DIL_MAX_DIST = 128

LANES = 128
BLK = 128
N_RES = 16
N_J = SEQ // N_RES
N_BLK = SEQ // BLK
ROWS_D4 = BLK // 4
ROWS_D1 = BLK // N_RES
GROUP = 1
N_GROUPS = N_BLK // GROUP
LOG2E = math.log2(math.e)
VMEM_LIMIT = 56 * 1024 * 1024


def _rms(x, g):
    return x * lax.rsqrt(jnp.mean(x * x, axis=-1, keepdims=True) + EPS) * g


def _dist_patterns():
    inf = np.float32(np.inf)
    j = np.arange(BLK)
    d16 = (j[:, None] - j[None, :]).astype(np.float32)
    d16 = np.where(d16 >= 0, 16.0 * d16, inf).astype(np.float32)

    t = np.arange(BLK) // ROWS_D4
    jj = np.arange(BLK) % ROWS_D4
    sub = 4 * jj + t
    cur = sub[:, None] - sub[None, :]
    prev = cur + BLK
    d4 = np.concatenate([prev, cur], axis=1).astype(np.float32)
    d4 = np.where((d4 >= 0) & (d4 <= DIL_MAX_DIST), 4.0 * d4, inf).astype(np.float32)

    r = np.arange(BLK) // ROWS_D1
    jj = np.arange(BLK) % ROWS_D1
    pos = N_RES * jj + r
    cur = pos[:, None] - pos[None, :]
    prev = cur + BLK
    d1 = np.concatenate([prev, cur], axis=1).astype(np.float32)
    d1a = np.where((d1 >= 0) & (d1 <= DIL_MAX_DIST), d1, inf).astype(np.float32)
    d1b = np.where((d1 >= 0) & (d1 <= SWA_WINDOW - 1), d1, inf).astype(np.float32)
    return d16, d4, d1a, d1b


def _ones_pattern(nk):
    e = np.zeros((2 * nk, LANES), np.float32)
    e[:nk, :HEAD_DIM] = 1.0
    e[nk:, HEAD_DIM:] = 1.0
    return e


PERM_J, PERM_R = 64, 8


def _permute_norm_kernel(x_ref, g_ref, xp_ref, h_ref):
    x = jnp.swapaxes(x_ref[...], 0, 1)
    xp_ref[...] = x
    h_ref[...] = _rms(x, g_ref[...]).astype(BF16)


def _permute_norm(x, g):
    b = x.shape[0]
    nat = pl.BlockSpec((None, PERM_J, PERM_R, D_MODEL), lambda i, r, j: (i, j, r, 0))
    res = pl.BlockSpec((None, PERM_R, PERM_J, D_MODEL), lambda i, r, j: (i, r, j, 0))
    xp, h = pl.pallas_call(
        _permute_norm_kernel,
        grid=(b, N_RES // PERM_R, N_J // PERM_J),
        in_specs=[nat, pl.BlockSpec((1, 1, D_MODEL), lambda i, r, j: (0, 0, 0))],
        out_specs=[res, res],
        out_shape=[jax.ShapeDtypeStruct((b, N_RES, N_J, D_MODEL), F32),
                   jax.ShapeDtypeStruct((b, N_RES, N_J, D_MODEL), BF16)],
        compiler_params=pltpu.CompilerParams(
            dimension_semantics=("arbitrary",) * 3, vmem_limit_bytes=VMEM_LIMIT),
        name="permute_norm",
    )(x.reshape(b, N_J, N_RES, D_MODEL), g.reshape(1, 1, D_MODEL))
    return xp.reshape(b * SEQ, D_MODEL), h.reshape(b * SEQ, D_MODEL)


def _unpermute_kernel(y_ref, o_ref):
    o_ref[...] = jnp.swapaxes(y_ref[...], 0, 1)


def _unpermute(y, b):
    nat = pl.BlockSpec((None, PERM_J, PERM_R, D_MODEL), lambda i, r, j: (i, j, r, 0))
    res = pl.BlockSpec((None, PERM_R, PERM_J, D_MODEL), lambda i, r, j: (i, r, j, 0))
    out = pl.pallas_call(
        _unpermute_kernel,
        grid=(b, N_RES // PERM_R, N_J // PERM_J),
        in_specs=[res],
        out_specs=nat,
        out_shape=jax.ShapeDtypeStruct((b, N_J, N_RES, D_MODEL), F32),
        compiler_params=pltpu.CompilerParams(
            dimension_semantics=("arbitrary",) * 3, vmem_limit_bytes=VMEM_LIMIT),
        name="unpermute",
    )(y.reshape(b, N_RES, N_J, D_MODEL))
    return out.reshape(b, SEQ, D_MODEL)


IN_PROJ_RES = 4
IN_PROJ_TM = IN_PROJ_RES * N_J
IN_PROJ_STEPS = N_RES // IN_PROJ_RES
Q_SCALE = LOG2E * HEAD_DIM ** -0.5


N_IN_PROJ_OUT = 12


def _in_proj_kernel(h_ref, w_ref, *rest):
    n_cast = (len(rest) - N_IN_PROJ_OUT) // 2
    cast_src, cast_dst = rest[:n_cast], rest[n_cast + N_IN_PROJ_OUT:]
    q16, k16l, k16h, v16l, v16h, q1, k1l, k1h, v1l, v1h, qb1, kvb1 = rest[n_cast:n_cast + N_IN_PROJ_OUT]
    h = h_ref[...]

    def order1(x):
        n = x.shape[1]
        pieces = [x[r * N_J + jb * ROWS_D1:r * N_J + (jb + 1) * ROWS_D1]
                  for jb in range(N_BLK) for r in range(IN_PROJ_RES)]
        return jnp.concatenate(pieces, axis=0).reshape(N_BLK, ROWS_D1 * IN_PROJ_RES, n)

    def proj(col, width, scale=None):
        x = jnp.dot(h, w_ref[:, col:col + width], preferred_element_type=F32)
        return x if scale is None else x * scale

    def store2(x, o16, o1):
        o16[...] = x.astype(BF16)
        o1[...] = order1(x).astype(BF16)

    def first_head(width):
        return (lax.broadcasted_iota(jnp.int32, (IN_PROJ_TM, width), 1) & HEAD_DIM) == 0

    store2(proj(0, D_A, Q_SCALE), q16, q1)
    lo = first_head(D_A)
    for i, outs in ((1, ((k16l, k1l), (k16h, k1h))), (2, ((v16l, v1l), (v16h, v1h)))):
        x = proj(i * D_A, D_A)
        store2(jnp.where(lo, x, 0.0), *outs[0])
        store2(jnp.where(lo, 0.0, x), *outs[1])
    qb1[...] = order1(proj(3 * D_A, D_B, Q_SCALE)).astype(BF16)

    x = proj(3 * D_A + D_B, 2 * D_KV_B)
    lo = first_head(LANES)
    tiles = []
    for part in (x[:, :LANES], x[:, LANES:]):
        swapped = pltpu.roll(part, HEAD_DIM, 1)
        tiles += [jnp.where(lo, part, 0.0), jnp.where(lo, 0.0, swapped),
                  jnp.where(lo, swapped, 0.0), jnp.where(lo, 0.0, part)]
    kvb1[...] = order1(jnp.concatenate(tiles, axis=1)).astype(BF16)
    _cast_chunks(cast_src, cast_dst)


def _in_proj(h, w, b, nxt):
    t = h.shape[0]
    steps = t // IN_PROJ_TM
    per = IN_PROJ_STEPS
    o16 = pl.BlockSpec((IN_PROJ_TM, D_A), lambda m: (m, 0))
    o1 = lambda c: pl.BlockSpec((None, N_BLK, None, ROWS_D1 * IN_PROJ_RES, c),
                                lambda m: (m // per, 0, m % per, 0, 0))
    s16 = jax.ShapeDtypeStruct((t, D_A), BF16)
    s1 = lambda c: jax.ShapeDtypeStruct((b, N_BLK, per, ROWS_D1 * IN_PROJ_RES, c), BF16)
    in_specs = [pl.BlockSpec((IN_PROJ_TM, D_MODEL), lambda m: (m, 0)),
                pl.BlockSpec((D_MODEL, IN_COLS), lambda m: (0, 0), pipeline_mode=pl.Buffered(1))]
    out_specs = [o16] * 5 + [o1(D_A)] * 5 + [o1(D_B), o1(8 * LANES)]
    out_shape = [s16] * 5 + [s1(D_A)] * 5 + [s1(D_B), s1(8 * LANES)]
    args = [h, w]
    if nxt is not None:
        wf, layer = nxt
        chunk = (wf.shape[1] // steps, wf.shape[2])
        in_specs.append(pl.BlockSpec((None,) + chunk, lambda m: (layer, m, 0)))
        out_specs.append(pl.BlockSpec(chunk, lambda m: (m, 0)))
        out_shape.append(jax.ShapeDtypeStruct(wf.shape[1:], BF16))
        args.append(wf)
    outs = pl.pallas_call(
        _in_proj_kernel,
        grid=(steps,),
        in_specs=in_specs,
        out_specs=out_specs,
        out_shape=out_shape,
        compiler_params=pltpu.CompilerParams(
            dimension_semantics=("arbitrary",), vmem_limit_bytes=VMEM_LIMIT),
        name="in_proj",
    )(*args)
    qkv, cast = outs[:N_IN_PROJ_OUT], outs[N_IN_PROJ_OUT:]
    return [o.reshape(b, SEQ, o.shape[-1]) for o in qkv], list(cast)


def _lane_lo():
    return lax.broadcasted_iota(jnp.int32, (BLK, LANES), 1) < HEAD_DIM


def _rows(ref, chunks):
    return jnp.concatenate([ref[pl.ds(start, size), :] for start, size in chunks], axis=0)


def _window(pair, window, src):
    chunks = [c for blk in window for c in src(blk)]
    return jnp.concatenate([_rows(ref, chunks) for ref in pair], axis=0)


def _scores_softmax(qs, ks, src, blocks, windows, bias0, bias1):
    nk = BLK * len(windows[0])
    q = jnp.stack([_rows(qs, src(blk)) for blk in blocks])
    kk = jnp.stack([_window(ks, w, src) for w in windows])
    s = jnp.einsum("gqd,gkd->gqk", q, kk, preferred_element_type=F32)
    l0 = s[:, :, :nk] + bias0
    l1 = s[:, :, nk:] + bias1
    m0 = jnp.max(l0, axis=2, keepdims=True)
    m1 = jnp.max(l1, axis=2, keepdims=True)
    p = jnp.concatenate([jnp.exp2(l0 - m0), jnp.exp2(l1 - m1)], axis=2).astype(BF16)
    return p, m0, m1


def _weighted_values(p, vs, ones_ref, src, windows):
    ones = ones_ref[...]
    vv = jnp.stack([jnp.concatenate([_window(vs, w, src), ones], axis=1) for w in windows])
    na = jnp.einsum("gqk,gkd->gqd", p, vv, preferred_element_type=F32)
    return na[:, :, :LANES], na[:, :, LANES:]


def _is_first(blk, per):
    return per is not None and blk % per == 0


def _band_windows(blocks, per):
    if per is None:
        return [(blk,) for blk in blocks]
    return [(blk, blk + 1) if _is_first(blk, per) else (blk - 1, blk) for blk in blocks]


def _band_bias(blocks, per, bias, bias_first):
    if not any(_is_first(blk, per) for blk in blocks):
        return bias[0], bias[1]
    return tuple(jnp.stack([(bias_first if _is_first(blk, per) else bias)[h] for blk in blocks])
                 for h in range(2))


def _pipelined(n_groups, first_half, second_half):
    first_half(0, 0)
    for g in range(n_groups - 1):
        first_half(g + 1, (g + 1) % 2)
        second_half(g, g % 2)
    second_half(n_groups - 1, (n_groups - 1) % 2)


def _first_block_bias(bias, bias_first):
    for h in range(2):
        bias_first[h, :, :BLK] = bias[h, :, BLK:]
        bias_first[h, :, BLK:] = jnp.full((BLK, BLK), -jnp.inf, F32)


def _dilated_branches(slopes_ref, hp, lo, q16, k16, v16, q1, k1, v1, dists, ones, biases,
                      num_s, m_s, den_s, p_scr):
    d16_ref, d4_ref, d1_ref = dists
    e1_ref, e2_ref = ones
    b16, b4, b4f, b1, b1f = biases
    ns0 = -LOG2E * slopes_ref[2 * hp]
    ns1 = -LOG2E * slopes_ref[2 * hp + 1]
    for bias, dist in ((b16, d16_ref), (b4, d4_ref), (b1, d1_ref)):
        bias[0] = ns0 * dist[...]
        bias[1] = ns1 * dist[...]
    _first_block_bias(b4, b4f)
    _first_block_bias(b1, b1f)

    def chunks16(blk):
        return [(blk * BLK, BLK)]

    def chunks4(blk):
        r4, mb = divmod(blk, 4)
        return [((r4 + 4 * t) * N_J + ROWS_D4 * mb, ROWS_D4) for t in range(4)]

    def chunks1_residue_major(blk):
        return [(r * N_J + ROWS_D1 * blk, ROWS_D1) for r in range(N_RES)]

    def scatter(dst_scr, branch, x, chunks):
        row = 0
        for start, size in chunks:
            dst_scr[branch, pl.ds(start, size), :] = x[row:row + size]
            row += size

    def blocks_of(g):
        return [g * GROUP + i for i in range(GROUP)]

    def gather(src_scr, slot, chunks):
        return jnp.concatenate([src_scr[slot, pl.ds(start, size), :] for start, size in chunks],
                               axis=0)

    def branch_halves(qs, ks, vs, ones_ref, src, dst, bias, bias_first, per, fold):
        def first_half(g, slot):
            blocks = blocks_of(g)
            windows = _band_windows(blocks, per)
            bias0, bias1 = _band_bias(blocks, per, bias, bias_first)
            p, m0, m1 = _scores_softmax(qs, ks, src, blocks, windows, bias0, bias1)
            p_scr[slot, :, :, :p.shape[2]] = p
            for i, blk in enumerate(blocks):
                scatter(m_s, 1 if fold else 0, jnp.where(lo, m0[i], m1[i]), dst(blk))

        def second_half(g, slot):
            blocks = blocks_of(g)
            windows = _band_windows(blocks, per)
            width = 2 * BLK * len(windows[0])
            num, den = _weighted_values(p_scr[slot, :, :, :width], vs, ones_ref, src, windows)
            for i, blk in enumerate(blocks):
                rows = dst(blk)
                n, d = num[i], den[i]
                if fold:
                    m_new, m_old = gather(m_s, 1, rows), gather(m_s, 0, rows)
                    m_all = jnp.maximum(m_new, m_old)
                    w_new, w_old = jnp.exp2(m_new - m_all), jnp.exp2(m_old - m_all)
                    n = n * w_new + gather(num_s, 0, rows) * w_old
                    d = d * w_new + gather(den_s, 0, rows) * w_old
                    scatter(m_s, 0, m_all, rows)
                scatter(num_s, 0, n, rows)
                scatter(den_s, 0, d, rows)

        return first_half, second_half

    _pipelined(N_GROUPS, *branch_halves(q16, k16, v16, e1_ref, chunks16, chunks16,
                                        b16, None, None, False))
    _pipelined(N_GROUPS, *branch_halves(q16, k16, v16, e2_ref, chunks4, chunks4,
                                        b4, b4f, 4, True))
    _pipelined(N_GROUPS, *branch_halves(q1, k1, v1, e2_ref, chunks16, chunks1_residue_major,
                                        b1, b1f, N_BLK, True))


def _sliding_window(slopes_ref, sinks_ref, hp, lo, q1, k1, v1, d1_ref, e2_ref, b1, b1f,
                    o_scr, p_scr, m_scr):
    sink = jnp.where(lo, LOG2E * sinks_ref[2 * hp], LOG2E * sinks_ref[2 * hp + 1])
    b1[0] = (-LOG2E * slopes_ref[2 * hp]) * d1_ref[...]
    b1[1] = (-LOG2E * slopes_ref[2 * hp + 1]) * d1_ref[...]
    _first_block_bias(b1, b1f)

    def src(blk):
        return [(blk * BLK, BLK)]

    def blocks_of(g):
        return [g * GROUP + i for i in range(GROUP)]

    def first_half(g, slot):
        blocks = blocks_of(g)
        bias0, bias1 = _band_bias(blocks, N_BLK, b1, b1f)
        p, m0, m1 = _scores_softmax(q1, k1, src, blocks, _band_windows(blocks, N_BLK),
                                    bias0, bias1)
        p_scr[slot] = p
        for i in range(GROUP):
            m_scr[slot, i] = jnp.where(lo, m0[i], m1[i])

    def second_half(g, slot):
        blocks = blocks_of(g)
        num, den = _weighted_values(p_scr[slot], v1, e2_ref, src, _band_windows(blocks, N_BLK))
        for i, blk in enumerate(blocks):
            m = m_scr[slot, i]
            m2 = jnp.maximum(m, sink)
            c = jnp.exp2(m - m2)
            out = (num[i] * c) / (den[i] * c + jnp.exp2(sink - m2))
            for r in range(N_RES):
                o_scr[pl.ds(r * N_J + ROWS_D1 * blk, ROWS_D1), :] = (
                    out[r * ROWS_D1:(r + 1) * ROWS_D1])

    _pipelined(N_GROUPS, first_half, second_half)


N_ATTN_IN, N_ATTN_OUT, N_ATTN_SCRATCH = 24, 2, 14


def _attn_kernel(*refs):
    n_cast = (len(refs) - N_ATTN_IN - N_ATTN_OUT - N_ATTN_SCRATCH) // 2
    (slopes_a_ref, slopes_b_ref, sinks_ref, q16, k16l, k16h, v16l, v16h, q1, k1l, k1h, v1l, v1h,
     qb, kbl, kbh, vbl, vbh, d16_ref, d4_ref, d1a_ref, d1b_ref, e1_ref, e2_ref) = refs[:N_ATTN_IN]
    cast_src = refs[N_ATTN_IN:N_ATTN_IN + n_cast]
    oa_ref, ob_ref = refs[N_ATTN_IN + n_cast:N_ATTN_IN + n_cast + N_ATTN_OUT]
    cast_dst = refs[N_ATTN_IN + n_cast + N_ATTN_OUT:len(refs) - N_ATTN_SCRATCH]
    (b16, b4, b4f, b1, b1f, bb, bbf, num_s, m_s, den_s, pa_scr, pb_scr, mb_scr,
     ob_scr) = refs[len(refs) - N_ATTN_SCRATCH:]
    hp = pl.program_id(1)
    lo = _lane_lo()
    _dilated_branches(slopes_a_ref, hp, lo, q16, (k16l, k16h), (v16l, v16h),
                      q1, (k1l, k1h), (v1l, v1h), (d16_ref, d4_ref, d1a_ref),
                      (e1_ref, e2_ref), (b16, b4, b4f, b1, b1f), num_s, m_s, den_s, pa_scr)
    _sliding_window(slopes_b_ref, sinks_ref, hp, lo, qb, (kbl, kbh), (vbl, vbh),
                    d1b_ref, e2_ref, bb, bbf, ob_scr, pb_scr, mb_scr)
    _cast_chunks(cast_src, cast_dst)

    def finish(g, carry):
        rows = pl.ds(pl.multiple_of(g * BLK, BLK), BLK)
        oa_ref[rows, :] = (num_s[0, rows, :] / den_s[0, rows, :]).astype(oa_ref.dtype)
        ob_ref[rows, :] = ob_scr[rows, :].astype(ob_ref.dtype)
        return carry

    lax.fori_loop(0, N_BLK, finish, 0, unroll=2)


def _attention(qkv_a, qb1, kvb1, slopes_a, slopes_b, sinks, d16, d4, d1a, d1b, e1, e2, nxt):
    b = qb1.shape[0]
    n_pairs = N_HEADS_A // 2
    assert N_HEADS_B // 2 == n_pairs
    pairs_per_kv = n_pairs // N_KV_B
    smem = pl.BlockSpec(memory_space=pltpu.SMEM)
    col = pl.BlockSpec((None, SEQ, LANES), lambda i, hp: (i, 0, hp))
    kv_tile = lambda tile: pl.BlockSpec(
        (None, SEQ, LANES), lambda i, hp, tile=tile: (i, 0, tile + 2 * (hp // pairs_per_kv)))
    const = lambda shape: pl.BlockSpec(shape, lambda i, hp: (0, 0))
    band = pltpu.VMEM((2, BLK, 2 * BLK), F32)
    stat = pltpu.VMEM((2, SEQ, LANES), F32)
    probs = pltpu.VMEM((2, GROUP, BLK, 4 * BLK), BF16)
    in_specs = [smem, smem, smem] + [col] * 11 + [
        kv_tile(0), kv_tile(1), kv_tile(4), kv_tile(5),
        const((BLK, BLK)), const((BLK, 2 * BLK)), const((BLK, 2 * BLK)),
        const((BLK, 2 * BLK)), const((2 * BLK, LANES)), const((4 * BLK, LANES))]
    out_specs = [col, col]
    out_shape = [jax.ShapeDtypeStruct((b, SEQ, D_A), BF16),
                 jax.ShapeDtypeStruct((b, SEQ, D_B), BF16)]
    args = [slopes_a, slopes_b, sinks, *qkv_a, qb1, kvb1, kvb1, kvb1, kvb1,
            d16, d4, d1a, d1b, e1, e2]
    assert len(in_specs) == len(args) == N_ATTN_IN
    if nxt is not None:
        *stacked, layer = nxt
        for wf in stacked:
            chunk = (wf.shape[1] // (b * n_pairs), wf.shape[2])
            in_specs.append(pl.BlockSpec((None,) + chunk,
                                         lambda i, hp: (layer, i * n_pairs + hp, 0)))
            out_specs.append(pl.BlockSpec(chunk, lambda i, hp: (i * n_pairs + hp, 0)))
            out_shape.append(jax.ShapeDtypeStruct(wf.shape[1:], BF16))
        args += stacked
    return pl.pallas_call(
        _attn_kernel,
        grid=(b, n_pairs),
        in_specs=in_specs,
        out_specs=out_specs,
        out_shape=out_shape,
        scratch_shapes=[pltpu.VMEM((2, BLK, BLK), F32), band, band, band, band, band, band,
                        stat, stat, stat, probs, probs,
                        pltpu.VMEM((2, GROUP, BLK, LANES), F32),
                        pltpu.VMEM((SEQ, LANES), F32)],
        compiler_params=pltpu.CompilerParams(
            dimension_semantics=("arbitrary", "arbitrary"), vmem_limit_bytes=VMEM_LIMIT),
        name="attention",
    )(*args)


def _cast_chunks(src_refs, dst_refs):
    for src, dst in zip(src_refs, dst_refs):
        dst[...] = src[...].astype(BF16)


def _out_proj_kernel(a_ref, b_ref, x_ref, ga_ref, gb_ref, w_ref, gf_ref, xo_ref, h_ref):
    na = _rms(a_ref[...].astype(F32), ga_ref[...]).astype(BF16)
    nb = _rms(b_ref[...].astype(F32), gb_ref[...]).astype(BF16)
    mix = jnp.concatenate([na, nb], axis=1)
    x = x_ref[...] + jnp.dot(mix, w_ref[...], preferred_element_type=F32)
    xo_ref[...] = x
    h_ref[...] = _rms(x, gf_ref[...]).astype(BF16)


def _out_proj(a, b, x, ga, gb, w, gf):
    t = x.shape[0]
    tm = 512
    row = lambda n: pl.BlockSpec((tm, n), lambda m: (m, 0))
    const = lambda r, c: pl.BlockSpec((r, c), lambda m: (0, 0))
    return pl.pallas_call(
        _out_proj_kernel,
        grid=(t // tm,),
        in_specs=[row(D_A), row(D_B), row(D_MODEL), const(1, D_A), const(1, D_B),
                  pl.BlockSpec((D_MIX, D_MODEL), lambda m: (0, 0), pipeline_mode=pl.Buffered(1)),
                  const(1, D_MODEL)],
        out_specs=[row(D_MODEL), row(D_MODEL)],
        out_shape=[jax.ShapeDtypeStruct((t, D_MODEL), F32),
                   jax.ShapeDtypeStruct((t, D_MODEL), BF16)],
        compiler_params=pltpu.CompilerParams(
            dimension_semantics=("arbitrary",), vmem_limit_bytes=VMEM_LIMIT),
        name="out_proj",
    )(a, b, x, ga.reshape(1, D_A), gb.reshape(1, D_B), w, gf.reshape(1, D_MODEL))


GATE_UP_TM, GATE_UP_TF = 2048, 512
DOWN_TM = 256


def _gate_up_kernel(h_ref, wg_ref, wu_ref, *rest):
    n_cast = (len(rest) - 1) // 2
    cast_src, act_ref, cast_dst = rest[:n_cast], rest[n_cast], rest[n_cast + 1:]
    h = h_ref[...]
    gate = jnp.dot(h, wg_ref[...], preferred_element_type=F32)
    up = jnp.dot(h, wu_ref[...], preferred_element_type=F32)
    act_ref[...] = (jax.nn.silu(gate) * up).astype(BF16)
    _cast_chunks(cast_src, cast_dst)


def _gate_up(h, wg, wu, nxt):
    t = h.shape[0]
    tm, tf = GATE_UP_TM, GATE_UP_TF
    n_m, n_f = t // tm, D_FF // tf
    w_spec = pl.BlockSpec((D_MODEL, tf), lambda m, f: (0, f))
    in_specs = [pl.BlockSpec((tm, D_MODEL), lambda m, f: (m, 0)), w_spec, w_spec]
    out_specs = [pl.BlockSpec((tm, tf), lambda m, f: (m, f))]
    out_shape = [jax.ShapeDtypeStruct((t, D_FF), BF16)]
    args = [h, wg, wu]
    if nxt is not None:
        *stacked, layer = nxt
        chunk = (D_MODEL // n_m, D_FF // n_f)
        for wf in stacked:
            in_specs.append(pl.BlockSpec((None,) + chunk, lambda m, f: (layer, m, f)))
            out_specs.append(pl.BlockSpec(chunk, lambda m, f: (m, f)))
            out_shape.append(jax.ShapeDtypeStruct(wf.shape[1:], BF16))
        args += stacked
    return pl.pallas_call(
        _gate_up_kernel,
        grid=(n_m, n_f),
        in_specs=in_specs,
        out_specs=out_specs,
        out_shape=out_shape,
        compiler_params=pltpu.CompilerParams(
            dimension_semantics=("arbitrary", "arbitrary"), vmem_limit_bytes=VMEM_LIMIT),
        name="ffn_gate_up",
    )(*args)


def _down_kernel(act_ref, wd_ref, x_ref, gn_ref, *rest):
    n_cast = (len(rest) - 2) // 2
    cast_src, (xo_ref, hn_ref), cast_dst = rest[:n_cast], rest[n_cast:n_cast + 2], rest[n_cast + 2:]
    x = x_ref[...] + jnp.dot(act_ref[...], wd_ref[...], preferred_element_type=F32)
    xo_ref[...] = x
    hn_ref[...] = _rms(x, gn_ref[...]).astype(hn_ref.dtype)
    _cast_chunks(cast_src, cast_dst)


def _down(act, wd, x, gn, normed_dtype, nxt):
    t = x.shape[0]
    tm = DOWN_TM
    steps = t // tm
    row = lambda n: pl.BlockSpec((tm, n), lambda m: (m, 0))
    in_specs = [row(D_FF),
                pl.BlockSpec((D_FF, D_MODEL), lambda m: (0, 0), pipeline_mode=pl.Buffered(1)),
                row(D_MODEL), pl.BlockSpec((1, D_MODEL), lambda m: (0, 0))]
    out_specs = [row(D_MODEL), row(D_MODEL)]
    out_shape = [jax.ShapeDtypeStruct((t, D_MODEL), F32),
                 jax.ShapeDtypeStruct((t, D_MODEL), normed_dtype)]
    args = [act, wd, x, gn.reshape(1, D_MODEL)]
    if nxt is not None:
        *stacked, layer = nxt
        for wf in stacked:
            chunk = (wf.shape[1] // steps, wf.shape[2])
            in_specs.append(pl.BlockSpec((None,) + chunk, lambda m: (layer, m, 0)))
            out_specs.append(pl.BlockSpec(chunk, lambda m: (m, 0)))
            out_shape.append(jax.ShapeDtypeStruct(wf.shape[1:], BF16))
        args += stacked
    return pl.pallas_call(
        _down_kernel,
        grid=(steps,),
        in_specs=in_specs,
        out_specs=out_specs,
        out_shape=out_shape,
        compiler_params=pltpu.CompilerParams(
            dimension_semantics=("arbitrary",), vmem_limit_bytes=VMEM_LIMIT),
        name="ffn_down",
    )(*args)


def kernel(x, attn_norm, w_in, sinks, out_norm_a, out_norm_b, w_out,
           ffn_norm, w_gate, w_up, w_down, final_norm):
    b, s_len, d = x.shape
    assert (s_len, d) == (SEQ, D_MODEL)
    depth = w_in.shape[0]

    idx = jnp.arange(N_ALIBI, dtype=F32)
    slopes = jnp.exp2(-8.0 * (idx + 1.0) / N_ALIBI)
    slopes_a, slopes_b = slopes[0::2], slopes[1::2]
    d16, d4, d1a, d1b = (jnp.asarray(p) for p in _dist_patterns())
    e1 = jnp.asarray(_ones_pattern(BLK), BF16)
    e2 = jnp.asarray(_ones_pattern(2 * BLK), BF16)

    wi = w_in[0].astype(BF16)
    xp, h = _permute_norm(x, attn_norm[0])
    for l in range(depth):
        first, last = l == 0, l + 1 == depth
        (*qkv_a, qb1, kvb1), cast = _in_proj(h, wi, b, (w_out, 0) if first else None)
        if first:
            (wo,) = cast
        out_a, out_b, *ffn_w = _attention(qkv_a, qb1, kvb1, slopes_a, slopes_b, sinks[l],
                                          d16, d4, d1a, d1b, e1, e2,
                                          (w_gate, w_up, w_down, 0) if first else None)
        if first:
            wg, wu, wd = ffn_w
        out_a, out_b = out_a.reshape(b * SEQ, D_A), out_b.reshape(b * SEQ, D_B)
        xp, h = _out_proj(out_a, out_b, xp, out_norm_a[l], out_norm_b[l], wo, ffn_norm[l])
        act, *nxt_gu = _gate_up(h, wg, wu, None if last else (w_gate, w_up, l + 1))
        xp, h, *nxt_d = _down(act, wd, xp, final_norm if last else attn_norm[l + 1],
                              F32 if last else BF16,
                              None if last else (w_down, w_in, w_out, l + 1))
        if not last:
            (wg, wu), (wd, wi, wo) = nxt_gu, nxt_d
    return _unpermute(h, b)
```

```python
import math

import numpy as np
import jax
import jax.numpy as jnp
from jax import lax
from jax.experimental import pallas as pl
from jax.experimental.pallas import tpu as pltpu

F32 = jnp.float32
BF16 = jnp.bfloat16

D_MODEL = 2048
SEQ = 2048
HEAD_DIM = 64
N_HEADS_A = 16
N_HEADS_B = 16
N_KV_B = 2
D_A = N_HEADS_A * HEAD_DIM
D_B = N_HEADS_B * HEAD_DIM
D_KV_B = N_KV_B * HEAD_DIM
D_MIX = D_A + D_B
IN_COLS = 3 * D_A + D_B + 2 * D_KV_B
N_ALIBI = N_HEADS_A + N_HEADS_B
D_FF = 5632
EPS = 1e-6
SWA_WINDOW = 128
DIL_MAX_DIST = 128

LANES = 128
BLK = 128
N_RES = 16
N_J = SEQ // N_RES
N_BLK = SEQ // BLK
ROWS_D4 = BLK // 4
ROWS_D1 = BLK // N_RES
GROUP = 1
N_GROUPS = N_BLK // GROUP
LOG2E = math.log2(math.e)
VMEM_LIMIT = 56 * 1024 * 1024


def _rms(x, g):
    return x * lax.rsqrt(jnp.mean(x * x, axis=-1, keepdims=True) + EPS) * g


def _dist_patterns():
    inf = np.float32(np.inf)
    j = np.arange(BLK)
    d16 = (j[:, None] - j[None, :]).astype(np.float32)
    d16 = np.where(d16 >= 0, 16.0 * d16, inf).astype(np.float32)

    t = np.arange(BLK) // ROWS_D4
    jj = np.arange(BLK) % ROWS_D4
    sub = 4 * jj + t
    cur = sub[:, None] - sub[None, :]
    prev = cur + BLK
    d4 = np.concatenate([prev, cur], axis=1).astype(np.float32)
    d4 = np.where((d4 >= 0) & (d4 <= DIL_MAX_DIST), 4.0 * d4, inf).astype(np.float32)

    r = np.arange(BLK) // ROWS_D1
    jj = np.arange(BLK) % ROWS_D1
    pos = N_RES * jj + r
    cur = pos[:, None] - pos[None, :]
    prev = cur + BLK
    d1 = np.concatenate([prev, cur], axis=1).astype(np.float32)
    d1a = np.where((d1 >= 0) & (d1 <= DIL_MAX_DIST), d1, inf).astype(np.float32)
    d1b = np.where((d1 >= 0) & (d1 <= SWA_WINDOW - 1), d1, inf).astype(np.float32)
    return d16, d4, d1a, d1b


def _ones_pattern(nk):
    e = np.zeros((2 * nk, LANES), np.float32)
    e[:nk, :HEAD_DIM] = 1.0
    e[nk:, HEAD_DIM:] = 1.0
    return e


PERM_J, PERM_R = 64, 8


def _permute_norm_kernel(x_ref, g_ref, xp_ref, h_ref):
    x = jnp.swapaxes(x_ref[...], 0, 1)
    xp_ref[...] = x
    h_ref[...] = _rms(x, g_ref[...]).astype(BF16)


def _permute_norm(x, g):
    b = x.shape[0]
    nat = pl.BlockSpec((None, PERM_J, PERM_R, D_MODEL), lambda i, r, j: (i, j, r, 0))
    res = pl.BlockSpec((None, PERM_R, PERM_J, D_MODEL), lambda i, r, j: (i, r, j, 0))
    xp, h = pl.pallas_call(
        _permute_norm_kernel,
        grid=(b, N_RES // PERM_R, N_J // PERM_J),
        in_specs=[nat, pl.BlockSpec((1, 1, D_MODEL), lambda i, r, j: (0, 0, 0))],
        out_specs=[res, res],
        out_shape=[jax.ShapeDtypeStruct((b, N_RES, N_J, D_MODEL), F32),
                   jax.ShapeDtypeStruct((b, N_RES, N_J, D_MODEL), BF16)],
        compiler_params=pltpu.CompilerParams(
            dimension_semantics=("arbitrary",) * 3, vmem_limit_bytes=VMEM_LIMIT),
        name="permute_norm",
    )(x.reshape(b, N_J, N_RES, D_MODEL), g.reshape(1, 1, D_MODEL))
    return xp.reshape(b * SEQ, D_MODEL), h.reshape(b * SEQ, D_MODEL)


def _unpermute_kernel(y_ref, o_ref):
    o_ref[...] = jnp.swapaxes(y_ref[...], 0, 1)


def _unpermute(y, b):
    nat = pl.BlockSpec((None, PERM_J, PERM_R, D_MODEL), lambda i, r, j: (i, j, r, 0))
    res = pl.BlockSpec((None, PERM_R, PERM_J, D_MODEL), lambda i, r, j: (i, r, j, 0))
    out = pl.pallas_call(
        _unpermute_kernel,
        grid=(b, N_RES // PERM_R, N_J // PERM_J),
        in_specs=[res],
        out_specs=nat,
        out_shape=jax.ShapeDtypeStruct((b, N_J, N_RES, D_MODEL), F32),
        compiler_params=pltpu.CompilerParams(
            dimension_semantics=("arbitrary",) * 3, vmem_limit_bytes=VMEM_LIMIT),
        name="unpermute",
    )(y.reshape(b, N_RES, N_J, D_MODEL))
    return out.reshape(b, SEQ, D_MODEL)


IN_PROJ_RES = 4
IN_PROJ_TM = IN_PROJ_RES * N_J
IN_PROJ_STEPS = N_RES // IN_PROJ_RES
Q_SCALE = LOG2E * HEAD_DIM ** -0.5


N_IN_PROJ_OUT = 12


def _in_proj_kernel(h_ref, w_ref, *rest):
    n_cast = (len(rest) - N_IN_PROJ_OUT) // 2
    cast_src, cast_dst = rest[:n_cast], rest[n_cast + N_IN_PROJ_OUT:]
    q16, k16l, k16h, v16l, v16h, q1, k1l, k1h, v1l, v1h, qb1, kvb1 = rest[n_cast:n_cast + N_IN_PROJ_OUT]
    h = h_ref[...]

    def order1(x):
        n = x.shape[1]
        pieces = [x[r * N_J + jb * ROWS_D1:r * N_J + (jb + 1) * ROWS_D1]
                  for jb in range(N_BLK) for r in range(IN_PROJ_RES)]
        return jnp.concatenate(pieces, axis=0).reshape(N_BLK, ROWS_D1 * IN_PROJ_RES, n)

    def proj(col, width, scale=None):
        x = jnp.dot(h, w_ref[:, col:col + width], preferred_element_type=F32)
        return x if scale is None else x * scale

    def store2(x, o16, o1):
        o16[...] = x.astype(BF16)
        o1[...] = order1(x).astype(BF16)

    def first_head(width):
        return (lax.broadcasted_iota(jnp.int32, (IN_PROJ_TM, width), 1) & HEAD_DIM) == 0

    store2(proj(0, D_A, Q_SCALE), q16, q1)
    lo = first_head(D_A)
    for i, outs in ((1, ((k16l, k1l), (k16h, k1h))), (2, ((v16l, v1l), (v16h, v1h)))):
        x = proj(i * D_A, D_A)
        store2(jnp.where(lo, x, 0.0), *outs[0])
        store2(jnp.where(lo, 0.0, x), *outs[1])
    qb1[...] = order1(proj(3 * D_A, D_B, Q_SCALE)).astype(BF16)

    x = proj(3 * D_A + D_B, 2 * D_KV_B)
    lo = first_head(LANES)
    tiles = []
    for part in (x[:, :LANES], x[:, LANES:]):
        swapped = pltpu.roll(part, HEAD_DIM, 1)
        tiles += [jnp.where(lo, part, 0.0), jnp.where(lo, 0.0, swapped),
                  jnp.where(lo, swapped, 0.0), jnp.where(lo, 0.0, part)]
    kvb1[...] = order1(jnp.concatenate(tiles, axis=1)).astype(BF16)
    _cast_chunks(cast_src, cast_dst)


def _in_proj(h, w, b, nxt):
    t = h.shape[0]
    steps = t // IN_PROJ_TM
    per = IN_PROJ_STEPS
    o16 = pl.BlockSpec((IN_PROJ_TM, D_A), lambda m: (m, 0))
    o1 = lambda c: pl.BlockSpec((None, N_BLK, None, ROWS_D1 * IN_PROJ_RES, c),
                                lambda m: (m // per, 0, m % per, 0, 0))
    s16 = jax.ShapeDtypeStruct((t, D_A), BF16)
    s1 = lambda c: jax.ShapeDtypeStruct((b, N_BLK, per, ROWS_D1 * IN_PROJ_RES, c), BF16)
    in_specs = [pl.BlockSpec((IN_PROJ_TM, D_MODEL), lambda m: (m, 0)),
                pl.BlockSpec((D_MODEL, IN_COLS), lambda m: (0, 0), pipeline_mode=pl.Buffered(1))]
    out_specs = [o16] * 5 + [o1(D_A)] * 5 + [o1(D_B), o1(8 * LANES)]
    out_shape = [s16] * 5 + [s1(D_A)] * 5 + [s1(D_B), s1(8 * LANES)]
    args = [h, w]
    if nxt is not None:
        wf, layer = nxt
        chunk = (wf.shape[1] // steps, wf.shape[2])
        in_specs.append(pl.BlockSpec((None,) + chunk, lambda m: (layer, m, 0)))
        out_specs.append(pl.BlockSpec(chunk, lambda m: (m, 0)))
        out_shape.append(jax.ShapeDtypeStruct(wf.shape[1:], BF16))
        args.append(wf)
    outs = pl.pallas_call(
        _in_proj_kernel,
        grid=(steps,),
        in_specs=in_specs,
        out_specs=out_specs,
        out_shape=out_shape,
        compiler_params=pltpu.CompilerParams(
            dimension_semantics=("arbitrary",), vmem_limit_bytes=VMEM_LIMIT),
        name="in_proj",
    )(*args)
    qkv, cast = outs[:N_IN_PROJ_OUT], outs[N_IN_PROJ_OUT:]
    return [o.reshape(b, SEQ, o.shape[-1]) for o in qkv], list(cast)


def _lane_lo():
    return lax.broadcasted_iota(jnp.int32, (BLK, LANES), 1) < HEAD_DIM


def _rows(ref, chunks):
    return jnp.concatenate([ref[pl.ds(start, size), :] for start, size in chunks], axis=0)


def _window(pair, window, src):
    chunks = [c for blk in window for c in src(blk)]
    return jnp.concatenate([_rows(ref, chunks) for ref in pair], axis=0)


def _scores_softmax(qs, ks, src, blocks, windows, bias0, bias1):
    nk = BLK * len(windows[0])
    q = jnp.stack([_rows(qs, src(blk)) for blk in blocks])
    kk = jnp.stack([_window(ks, w, src) for w in windows])
    s = jnp.einsum("gqd,gkd->gqk", q, kk, preferred_element_type=F32)
    l0 = s[:, :, :nk] + bias0
    l1 = s[:, :, nk:] + bias1
    m0 = jnp.max(l0, axis=2, keepdims=True)
    m1 = jnp.max(l1, axis=2, keepdims=True)
    p = jnp.concatenate([jnp.exp2(l0 - m0), jnp.exp2(l1 - m1)], axis=2).astype(BF16)
    return p, m0, m1


def _weighted_values(p, vs, ones_ref, src, windows):
    ones = ones_ref[...]
    vv = jnp.stack([jnp.concatenate([_window(vs, w, src), ones], axis=1) for w in windows])
    na = jnp.einsum("gqk,gkd->gqd", p, vv, preferred_element_type=F32)
    return na[:, :, :LANES], na[:, :, LANES:]


def _is_first(blk, per):
    return per is not None and blk % per == 0


def _band_windows(blocks, per):
    if per is None:
        return [(blk,) for blk in blocks]
    return [(blk, blk + 1) if _is_first(blk, per) else (blk - 1, blk) for blk in blocks]


def _band_bias(blocks, per, bias, bias_first):
    if not any(_is_first(blk, per) for blk in blocks):
        return bias[0], bias[1]
    return tuple(jnp.stack([(bias_first if _is_first(blk, per) else bias)[h] for blk in blocks])
                 for h in range(2))


PIPE_AHEAD = 2
PIPE_SLOTS = PIPE_AHEAD + 1


def _pipelined(n_groups, first_half, second_half):
    for g in range(min(PIPE_AHEAD, n_groups)):
        first_half(g, g % PIPE_SLOTS)
    for g in range(n_groups):
        if g + PIPE_AHEAD < n_groups:
            first_half(g + PIPE_AHEAD, (g + PIPE_AHEAD) % PIPE_SLOTS)
        second_half(g, g % PIPE_SLOTS)


def _first_block_bias(bias, bias_first):
    for h in range(2):
        bias_first[h, :, :BLK] = bias[h, :, BLK:]
        bias_first[h, :, BLK:] = jnp.full((BLK, BLK), -jnp.inf, F32)


def _dilated_branches(slopes_ref, hp, lo, q16, k16, v16, q1, k1, v1, dists, ones, biases,
                      num_s, m_s, den_s, p_scr):
    d16_ref, d4_ref, d1_ref = dists
    e1_ref, e2_ref = ones
    b16, b4, b4f, b1, b1f = biases
    ns0 = -LOG2E * slopes_ref[2 * hp]
    ns1 = -LOG2E * slopes_ref[2 * hp + 1]
    for bias, dist in ((b16, d16_ref), (b4, d4_ref), (b1, d1_ref)):
        bias[0] = ns0 * dist[...]
        bias[1] = ns1 * dist[...]
    _first_block_bias(b4, b4f)
    _first_block_bias(b1, b1f)

    def chunks16(blk):
        return [(blk * BLK, BLK)]

    def chunks4(blk):
        r4, mb = divmod(blk, 4)
        return [((r4 + 4 * t) * N_J + ROWS_D4 * mb, ROWS_D4) for t in range(4)]

    def chunks1_residue_major(blk):
        return [(r * N_J + ROWS_D1 * blk, ROWS_D1) for r in range(N_RES)]

    def scatter(dst_scr, branch, x, chunks):
        row = 0
        for start, size in chunks:
            dst_scr[branch, pl.ds(start, size), :] = x[row:row + size]
            row += size

    def blocks_of(g):
        return [g * GROUP + i for i in range(GROUP)]

    def gather(src_scr, slot, chunks):
        return jnp.concatenate([src_scr[slot, pl.ds(start, size), :] for start, size in chunks],
                               axis=0)

    def branch_halves(qs, ks, vs, ones_ref, src, dst, bias, bias_first, per, fold):
        def first_half(g, slot):
            blocks = blocks_of(g)
            windows = _band_windows(blocks, per)
            bias0, bias1 = _band_bias(blocks, per, bias, bias_first)
            p, m0, m1 = _scores_softmax(qs, ks, src, blocks, windows, bias0, bias1)
            p_scr[slot, :, :, :p.shape[2]] = p
            for i, blk in enumerate(blocks):
                scatter(m_s, 1 if fold else 0, jnp.where(lo, m0[i], m1[i]), dst(blk))

        def second_half(g, slot):
            blocks = blocks_of(g)
            windows = _band_windows(blocks, per)
            width = 2 * BLK * len(windows[0])
            num, den = _weighted_values(p_scr[slot, :, :, :width], vs, ones_ref, src, windows)
            for i, blk in enumerate(blocks):
                rows = dst(blk)
                n, d = num[i], den[i]
                if fold:
                    m_new, m_old = gather(m_s, 1, rows), gather(m_s, 0, rows)
                    m_all = jnp.maximum(m_new, m_old)
                    w_new, w_old = jnp.exp2(m_new - m_all), jnp.exp2(m_old - m_all)
                    n = n * w_new + gather(num_s, 0, rows) * w_old
                    d = d * w_new + gather(den_s, 0, rows) * w_old
                    scatter(m_s, 0, m_all, rows)
                scatter(num_s, 0, n, rows)
                scatter(den_s, 0, d, rows)

        return first_half, second_half

    _pipelined(N_GROUPS, *branch_halves(q16, k16, v16, e1_ref, chunks16, chunks16,
                                        b16, None, None, False))
    _pipelined(N_GROUPS, *branch_halves(q16, k16, v16, e2_ref, chunks4, chunks4,
                                        b4, b4f, 4, True))
    _pipelined(N_GROUPS, *branch_halves(q1, k1, v1, e2_ref, chunks16, chunks1_residue_major,
                                        b1, b1f, N_BLK, True))


def _sliding_window(slopes_ref, sinks_ref, hp, lo, q1, k1, v1, d1_ref, e2_ref, b1, b1f,
                    o_scr, p_scr, m_scr):
    sink = jnp.where(lo, LOG2E * sinks_ref[2 * hp], LOG2E * sinks_ref[2 * hp + 1])
    b1[0] = (-LOG2E * slopes_ref[2 * hp]) * d1_ref[...]
    b1[1] = (-LOG2E * slopes_ref[2 * hp + 1]) * d1_ref[...]
    _first_block_bias(b1, b1f)

    def src(blk):
        return [(blk * BLK, BLK)]

    def blocks_of(g):
        return [g * GROUP + i for i in range(GROUP)]

    def first_half(g, slot):
        blocks = blocks_of(g)
        bias0, bias1 = _band_bias(blocks, N_BLK, b1, b1f)
        p, m0, m1 = _scores_softmax(q1, k1, src, blocks, _band_windows(blocks, N_BLK),
                                    bias0, bias1)
        p_scr[slot] = p
        for i in range(GROUP):
            m_scr[slot, i] = jnp.where(lo, m0[i], m1[i])

    def second_half(g, slot):
        blocks = blocks_of(g)
        num, den = _weighted_values(p_scr[slot], v1, e2_ref, src, _band_windows(blocks, N_BLK))
        for i, blk in enumerate(blocks):
            m = m_scr[slot, i]
            m2 = jnp.maximum(m, sink)
            c = jnp.exp2(m - m2)
            out = (num[i] * c) / (den[i] * c + jnp.exp2(sink - m2))
            for r in range(N_RES):
                o_scr[pl.ds(r * N_J + ROWS_D1 * blk, ROWS_D1), :] = (
                    out[r * ROWS_D1:(r + 1) * ROWS_D1])

    _pipelined(N_GROUPS, first_half, second_half)


N_ATTN_IN, N_ATTN_OUT, N_ATTN_SCRATCH = 24, 2, 14


def _attn_kernel(*refs):
    n_cast = (len(refs) - N_ATTN_IN - N_ATTN_OUT - N_ATTN_SCRATCH) // 2
    (slopes_a_ref, slopes_b_ref, sinks_ref, q16, k16l, k16h, v16l, v16h, q1, k1l, k1h, v1l, v1h,
     qb, kbl, kbh, vbl, vbh, d16_ref, d4_ref, d1a_ref, d1b_ref, e1_ref, e2_ref) = refs[:N_ATTN_IN]
    cast_src = refs[N_ATTN_IN:N_ATTN_IN + n_cast]
    oa_ref, ob_ref = refs[N_ATTN_IN + n_cast:N_ATTN_IN + n_cast + N_ATTN_OUT]
    cast_dst = refs[N_ATTN_IN + n_cast + N_ATTN_OUT:len(refs) - N_ATTN_SCRATCH]
    (b16, b4, b4f, b1, b1f, bb, bbf, num_s, m_s, den_s, pa_scr, pb_scr, mb_scr,
     ob_scr) = refs[len(refs) - N_ATTN_SCRATCH:]
    hp = pl.program_id(1)
    lo = _lane_lo()
    _dilated_branches(slopes_a_ref, hp, lo, q16, (k16l, k16h), (v16l, v16h),
                      q1, (k1l, k1h), (v1l, v1h), (d16_ref, d4_ref, d1a_ref),
                      (e1_ref, e2_ref), (b16, b4, b4f, b1, b1f), num_s, m_s, den_s, pa_scr)
    _sliding_window(slopes_b_ref, sinks_ref, hp, lo, qb, (kbl, kbh), (vbl, vbh),
                    d1b_ref, e2_ref, bb, bbf, ob_scr, pb_scr, mb_scr)
    _cast_chunks(cast_src, cast_dst)

    def finish(g, carry):
        rows = pl.ds(pl.multiple_of(g * BLK, BLK), BLK)
        oa_ref[rows, :] = (num_s[0, rows, :] / den_s[0, rows, :]).astype(oa_ref.dtype)
        ob_ref[rows, :] = ob_scr[rows, :].astype(ob_ref.dtype)
        return carry

    lax.fori_loop(0, N_BLK, finish, 0, unroll=2)


def _attention(qkv_a, qb1, kvb1, slopes_a, slopes_b, sinks, d16, d4, d1a, d1b, e1, e2, nxt):
    b = qb1.shape[0]
    n_pairs = N_HEADS_A // 2
    assert N_HEADS_B // 2 == n_pairs
    pairs_per_kv = n_pairs // N_KV_B
    smem = pl.BlockSpec(memory_space=pltpu.SMEM)
    col = pl.BlockSpec((None, SEQ, LANES), lambda i, hp: (i, 0, hp))
    kv_tile = lambda tile: pl.BlockSpec(
        (None, SEQ, LANES), lambda i, hp, tile=tile: (i, 0, tile + 2 * (hp // pairs_per_kv)))
    const = lambda shape: pl.BlockSpec(shape, lambda i, hp: (0, 0))
    band = pltpu.VMEM((2, BLK, 2 * BLK), F32)
    stat = pltpu.VMEM((2, SEQ, LANES), F32)
    probs = pltpu.VMEM((PIPE_SLOTS, GROUP, BLK, 4 * BLK), BF16)
    in_specs = [smem, smem, smem] + [col] * 11 + [
        kv_tile(0), kv_tile(1), kv_tile(4), kv_tile(5),
        const((BLK, BLK)), const((BLK, 2 * BLK)), const((BLK, 2 * BLK)),
        const((BLK, 2 * BLK)), const((2 * BLK, LANES)), const((4 * BLK, LANES))]
    out_specs = [col, col]
    out_shape = [jax.ShapeDtypeStruct((b, SEQ, D_A), BF16),
                 jax.ShapeDtypeStruct((b, SEQ, D_B), BF16)]
    args = [slopes_a, slopes_b, sinks, *qkv_a, qb1, kvb1, kvb1, kvb1, kvb1,
            d16, d4, d1a, d1b, e1, e2]
    assert len(in_specs) == len(args) == N_ATTN_IN
    if nxt is not None:
        *stacked, layer = nxt
        for wf in stacked:
            chunk = (wf.shape[1] // (b * n_pairs), wf.shape[2])
            in_specs.append(pl.BlockSpec((None,) + chunk,
                                         lambda i, hp: (layer, i * n_pairs + hp, 0)))
            out_specs.append(pl.BlockSpec(chunk, lambda i, hp: (i * n_pairs + hp, 0)))
            out_shape.append(jax.ShapeDtypeStruct(wf.shape[1:], BF16))
        args += stacked
    return pl.pallas_call(
        _attn_kernel,
        grid=(b, n_pairs),
        in_specs=in_specs,
        out_specs=out_specs,
        out_shape=out_shape,
        scratch_shapes=[pltpu.VMEM((2, BLK, BLK), F32), band, band, band, band, band, band,
                        stat, stat, stat, probs, probs,
                        pltpu.VMEM((PIPE_SLOTS, GROUP, BLK, LANES), F32),
                        pltpu.VMEM((SEQ, LANES), F32)],
        compiler_params=pltpu.CompilerParams(
            dimension_semantics=("arbitrary", "arbitrary"), vmem_limit_bytes=VMEM_LIMIT),
        name="attention",
    )(*args)


def _cast_chunks(src_refs, dst_refs):
    for src, dst in zip(src_refs, dst_refs):
        dst[...] = src[...].astype(BF16)


def _out_proj_kernel(a_ref, b_ref, x_ref, ga_ref, gb_ref, w_ref, gf_ref, xo_ref, h_ref):
    na = _rms(a_ref[...].astype(F32), ga_ref[...]).astype(BF16)
    nb = _rms(b_ref[...].astype(F32), gb_ref[...]).astype(BF16)
    mix = jnp.concatenate([na, nb], axis=1)
    x = x_ref[...] + jnp.dot(mix, w_ref[...], preferred_element_type=F32)
    xo_ref[...] = x
    h_ref[...] = _rms(x, gf_ref[...]).astype(BF16)


def _out_proj(a, b, x, ga, gb, w, gf):
    t = x.shape[0]
    tm = 512
    row = lambda n: pl.BlockSpec((tm, n), lambda m: (m, 0))
    const = lambda r, c: pl.BlockSpec((r, c), lambda m: (0, 0))
    return pl.pallas_call(
        _out_proj_kernel,
        grid=(t // tm,),
        in_specs=[row(D_A), row(D_B), row(D_MODEL), const(1, D_A), const(1, D_B),
                  pl.BlockSpec((D_MIX, D_MODEL), lambda m: (0, 0), pipeline_mode=pl.Buffered(1)),
                  const(1, D_MODEL)],
        out_specs=[row(D_MODEL), row(D_MODEL)],
        out_shape=[jax.ShapeDtypeStruct((t, D_MODEL), F32),
                   jax.ShapeDtypeStruct((t, D_MODEL), BF16)],
        compiler_params=pltpu.CompilerParams(
            dimension_semantics=("arbitrary",), vmem_limit_bytes=VMEM_LIMIT),
        name="out_proj",
    )(a, b, x, ga.reshape(1, D_A), gb.reshape(1, D_B), w, gf.reshape(1, D_MODEL))


GATE_UP_TM, GATE_UP_TF = 2048, 512
DOWN_TM = 256


def _gate_up_kernel(h_ref, wg_ref, wu_ref, *rest):
    n_cast = (len(rest) - 1) // 2
    cast_src, act_ref, cast_dst = rest[:n_cast], rest[n_cast], rest[n_cast + 1:]
    h = h_ref[...]
    gate = jnp.dot(h, wg_ref[...], preferred_element_type=F32)
    up = jnp.dot(h, wu_ref[...], preferred_element_type=F32)
    act_ref[...] = (jax.nn.silu(gate) * up).astype(BF16)
    _cast_chunks(cast_src, cast_dst)


def _gate_up(h, wg, wu, nxt):
    t = h.shape[0]
    tm, tf = GATE_UP_TM, GATE_UP_TF
    n_m, n_f = t // tm, D_FF // tf
    w_spec = pl.BlockSpec((D_MODEL, tf), lambda m, f: (0, f))
    in_specs = [pl.BlockSpec((tm, D_MODEL), lambda m, f: (m, 0)), w_spec, w_spec]
    out_specs = [pl.BlockSpec((tm, tf), lambda m, f: (m, f))]
    out_shape = [jax.ShapeDtypeStruct((t, D_FF), BF16)]
    args = [h, wg, wu]
    if nxt is not None:
        *stacked, layer = nxt
        chunk = (D_MODEL // n_m, D_FF // n_f)
        for wf in stacked:
            in_specs.append(pl.BlockSpec((None,) + chunk, lambda m, f: (layer, m, f)))
            out_specs.append(pl.BlockSpec(chunk, lambda m, f: (m, f)))
            out_shape.append(jax.ShapeDtypeStruct(wf.shape[1:], BF16))
        args += stacked
    return pl.pallas_call(
        _gate_up_kernel,
        grid=(n_m, n_f),
        in_specs=in_specs,
        out_specs=out_specs,
        out_shape=out_shape,
        compiler_params=pltpu.CompilerParams(
            dimension_semantics=("arbitrary", "arbitrary"), vmem_limit_bytes=VMEM_LIMIT),
        name="ffn_gate_up",
    )(*args)


def _down_kernel(act_ref, wd_ref, x_ref, gn_ref, *rest):
    n_cast = (len(rest) - 2) // 2
    cast_src, (xo_ref, hn_ref), cast_dst = rest[:n_cast], rest[n_cast:n_cast + 2], rest[n_cast + 2:]
    x = x_ref[...] + jnp.dot(act_ref[...], wd_ref[...], preferred_element_type=F32)
    xo_ref[...] = x
    hn_ref[...] = _rms(x, gn_ref[...]).astype(hn_ref.dtype)
    _cast_chunks(cast_src, cast_dst)


def _down(act, wd, x, gn, normed_dtype, nxt):
    t = x.shape[0]
    tm = DOWN_TM
    steps = t // tm
    row = lambda n: pl.BlockSpec((tm, n), lambda m: (m, 0))
    in_specs = [row(D_FF),
                pl.BlockSpec((D_FF, D_MODEL), lambda m: (0, 0), pipeline_mode=pl.Buffered(1)),
                row(D_MODEL), pl.BlockSpec((1, D_MODEL), lambda m: (0, 0))]
    out_specs = [row(D_MODEL), row(D_MODEL)]
    out_shape = [jax.ShapeDtypeStruct((t, D_MODEL), F32),
                 jax.ShapeDtypeStruct((t, D_MODEL), normed_dtype)]
    args = [act, wd, x, gn.reshape(1, D_MODEL)]
    if nxt is not None:
        *stacked, layer = nxt
        for wf in stacked:
            chunk = (wf.shape[1] // steps, wf.shape[2])
            in_specs.append(pl.BlockSpec((None,) + chunk, lambda m: (layer, m, 0)))
            out_specs.append(pl.BlockSpec(chunk, lambda m: (m, 0)))
            out_shape.append(jax.ShapeDtypeStruct(wf.shape[1:], BF16))
        args += stacked
    return pl.pallas_call(
        _down_kernel,
        grid=(steps,),
        in_specs=in_specs,
        out_specs=out_specs,
        out_shape=out_shape,
        compiler_params=pltpu.CompilerParams(
            dimension_semantics=("arbitrary",), vmem_limit_bytes=VMEM_LIMIT),
        name="ffn_down",
    )(*args)


def kernel(x, attn_norm, w_in, sinks, out_norm_a, out_norm_b, w_out,
           ffn_norm, w_gate, w_up, w_down, final_norm):
    b, s_len, d = x.shape
    assert (s_len, d) == (SEQ, D_MODEL)
    depth = w_in.shape[0]

    idx = jnp.arange(N_ALIBI, dtype=F32)
    slopes = jnp.exp2(-8.0 * (idx + 1.0) / N_ALIBI)
    slopes_a, slopes_b = slopes[0::2], slopes[1::2]
    d16, d4, d1a, d1b = (jnp.asarray(p) for p in _dist_patterns())
    e1 = jnp.asarray(_ones_pattern(BLK), BF16)
    e2 = jnp.asarray(_ones_pattern(2 * BLK), BF16)

    wi = w_in[0].astype(BF16)
    xp, h = _permute_norm(x, attn_norm[0])
    for l in range(depth):
        first, last = l == 0, l + 1 == depth
        (*qkv_a, qb1, kvb1), cast = _in_proj(h, wi, b, (w_out, 0) if first else None)
        if first:
            (wo,) = cast
        out_a, out_b, *ffn_w = _attention(qkv_a, qb1, kvb1, slopes_a, slopes_b, sinks[l],
                                          d16, d4, d1a, d1b, e1, e2,
                                          (w_gate, w_up, w_down, 0) if first else None)
        if first:
            wg, wu, wd = ffn_w
        out_a, out_b = out_a.reshape(b * SEQ, D_A), out_b.reshape(b * SEQ, D_B)
        xp, h = _out_proj(out_a, out_b, xp, out_norm_a[l], out_norm_b[l], wo, ffn_norm[l])
        act, *nxt_gu = _gate_up(h, wg, wu, None if last else (w_gate, w_up, l + 1))
        xp, h, *nxt_d = _down(act, wd, xp, final_norm if last else attn_norm[l + 1],
                              F32 if last else BF16,
                              None if last else (w_down, w_in, w_out, l + 1))
        if not last:
            (wg, wu), (wd, wi, wo) = nxt_gu, nxt_d
    return _unpermute(h, b)
```
